```python
import math
import jax, jax.numpy as jnp
from jax import lax
import numpy as np

D_MODEL = 1024
BATCH = 16
SEQ = 2048
DEPTH = 4

N_MIXERS = 2
N_A_LAYERS = (DEPTH + 1) // 2
N_B_LAYERS = DEPTH // 2

GLA_HEADS = 4
GLA_DK = (D_MODEL // 2) // GLA_HEADS
GLA_DV = D_MODEL // GLA_HEADS
GLA_GATE_RANK = 16
GLA_GATE_NORM = 16.0
GLA_CHUNK = 64
GLA_QK = GLA_HEADS * GLA_DK
GLA_V = GLA_HEADS * GLA_DV
GLA_IN = 2 * GLA_QK + 2 * GLA_V + 2 * GLA_GATE_RANK

DIFF_HEADS = 8
DIFF_DH = D_MODEL // DIFF_HEADS // 2
DIFF_QK = DIFF_HEADS * 2 * DIFF_DH
DIFF_V = DIFF_HEADS * 2 * DIFF_DH
DIFF_IN = 2 * DIFF_QK + DIFF_V
Q_BLOCK = 128

REL_BUCKETS = 32
REL_MAX_DIST = 128

D_FF = 2816
EPS = 1e-6

kernel_name = "hybrid_gla_diffattn_macaron_encoder"


def rms_norm(x, g):
    xf = x.astype(jnp.float32)
    y = xf * lax.rsqrt(jnp.mean(xf * xf, axis=-1, keepdims=True) + EPS)
    return (y * g.astype(jnp.float32)).astype(x.dtype)


def swiglu_ffn(h, w_gu, w_down):
    gate, up = jnp.split(h @ w_gu, 2, axis=-1)
    return (jax.nn.silu(gate) * up) @ w_down


def gla_chunked(q, k, v, log_a, strict):
    B, H, S, dk = q.shape
    dv = v.shape[-1]
    n = S // GLA_CHUNK

    def to_chunks(t):
        return jnp.moveaxis(t.reshape(B, H, n, GLA_CHUNK, t.shape[-1]), 2, 0)

    mask = jnp.tril(jnp.ones((GLA_CHUNK, GLA_CHUNK), dtype=bool), k=-1 if strict else 0)

    def step(state, inp):
        qc, kc, vc, gc = inp
        b = jnp.cumsum(gc, axis=-2)
        inter = jnp.einsum('bhcd,bhdv->bhcv', qc * jnp.exp(b), state)
        rel = jnp.minimum(b[:, :, :, None, :] - b[:, :, None, :, :], 0.0)
        decay = jnp.where(mask[:, :, None], jnp.exp(rel), 0.0)
        scores = jnp.sum(qc[:, :, :, None, :] * kc[:, :, None, :, :] * decay, axis=-1)
        intra = jnp.einsum('bhij,bhjv->bhiv', scores, vc)
        b_end = b[:, :, -1:, :]
        k_dec = kc * jnp.exp(b_end - b)
        new_state = jnp.exp(b_end[:, :, 0, :])[..., None] * state + jnp.einsum('bhcd,bhcv->bhdv', k_dec, vc)
        return new_state, inter + intra

    state0 = jnp.zeros((B, H, dk, dv), jnp.float32)
    _, out = lax.scan(step, state0, (to_chunks(q), to_chunks(k), to_chunks(v), to_chunks(log_a)))
    return jnp.moveaxis(out, 0, 2).reshape(B, H, S, dv)


def gla_mixer(h, w_in, w_gate2, b_gate, o_norm_g, w_out):
    B, S, _ = h.shape
    proj = h @ w_in
    q, k, v, g, lr = jnp.split(proj, [GLA_QK, 2 * GLA_QK, 2 * GLA_QK + GLA_V, 2 * GLA_QK + 2 * GLA_V], axis=-1)
    lr_f, lr_b = lr[..., :GLA_GATE_RANK], lr[..., GLA_GATE_RANK:]

    def heads(t, d):
        return t.reshape(B, S, GLA_HEADS, d).transpose(0, 2, 1, 3).astype(jnp.float32)

    def log_decay(lr_d, w2, bias):
        z = (lr_d @ w2 + bias).astype(jnp.float32)
        return jax.nn.log_sigmoid(z) / GLA_GATE_NORM

    qh = heads(q, GLA_DK) * (GLA_DK ** -0.5)
    kh = heads(k, GLA_DK)
    vh = heads(v, GLA_DV)
    la_f = heads(log_decay(lr_f, w_gate2[0], b_gate[0]), GLA_DK)
    la_b = heads(log_decay(lr_b, w_gate2[1], b_gate[1]), GLA_DK)

    flip = lambda t: jnp.flip(t, axis=2)
    o_f = gla_chunked(qh, kh, vh, la_f, False)
    o_b = flip(gla_chunked(flip(qh), flip(kh), flip(vh), flip(la_b), True))
    o = (o_f + o_b).transpose(0, 2, 1, 3)
    o = rms_norm(o, o_norm_g).reshape(B, S, GLA_V).astype(h.dtype) * jax.nn.silu(g)
    return o @ w_out


def t5_bucket(rel):
    nb = REL_BUCKETS // 2
    max_exact = nb // 2
    ret = (rel > 0).astype(jnp.int32) * nb
    n = jnp.abs(rel)
    nf = jnp.maximum(n, 1).astype(jnp.float32)
    large = max_exact + (jnp.log(nf / max_exact) / math.log(REL_MAX_DIST / max_exact) * (nb - max_exact)).astype(jnp.int32)
    large = jnp.minimum(large, nb - 1)
    return ret + jnp.where(n < max_exact, n, large)


def diff_mixer(h, w_in, qk_norm_g, lam_vecs, subln_g, w_out, rel_table, lam_init):
    B, S, _ = h.shape
    proj = h @ w_in
    q, k, v = jnp.split(proj, [DIFF_QK, 2 * DIFF_QK], axis=-1)
    q = rms_norm(q.reshape(B, S, DIFF_HEADS, 2, DIFF_DH), qk_norm_g[0]).astype(jnp.float32) * (DIFF_DH ** -0.5)
    k = rms_norm(k.reshape(B, S, DIFF_HEADS, 2, DIFF_DH), qk_norm_g[1]).astype(jnp.float32)
    q1 = q[..., 0, :].transpose(0, 2, 1, 3)
    q2 = q[..., 1, :].transpose(0, 2, 1, 3)
    k1 = k[..., 0, :].transpose(0, 2, 1, 3)
    k2 = k[..., 1, :].transpose(0, 2, 1, 3)
    vh = v.reshape(B, S, DIFF_HEADS, 2 * DIFF_DH).transpose(0, 2, 1, 3).astype(jnp.float32)

    lv = lam_vecs.astype(jnp.float32)
    lam = jnp.exp(jnp.sum(lv[0] * lv[1])) - jnp.exp(jnp.sum(lv[2] * lv[3])) + lam_init
    table = rel_table.astype(jnp.float32)
    nblk = S // Q_BLOCK
    kpos = jnp.arange(S, dtype=jnp.int32)

    def blocks(t):
        return jnp.moveaxis(t.reshape(B, DIFF_HEADS, nblk, Q_BLOCK, DIFF_DH), 2, 0)

    def attend(args):
        q1b, q2b, start = args
        qpos = start + jnp.arange(Q_BLOCK, dtype=jnp.int32)
        bias = jnp.transpose(table[t5_bucket(kpos[None, :] - qpos[:, None])], (2, 0, 1))
        p1 = jax.nn.softmax(jnp.einsum('bhqd,bhkd->bhqk', q1b, k1) + bias, axis=-1)
        p2 = jax.nn.softmax(jnp.einsum('bhqd,bhkd->bhqk', q2b, k2) + bias, axis=-1)
        return jnp.einsum('bhqk,bhkv->bhqv', p1 - lam * p2, vh)

    starts = jnp.arange(nblk, dtype=jnp.int32) * Q_BLOCK
    out = lax.map(attend, (blocks(q1), blocks(q2), starts))
    out = out.transpose(1, 0, 3, 2, 4).reshape(B, S, DIFF_HEADS, 2 * DIFF_DH)
    out = rms_norm(out, subln_g) * (1.0 - lam_init)
    return out.reshape(B, S, DIFF_V).astype(h.dtype) @ w_out


def setup_inputs(seed: int = 0) -> dict:
    key = jax.random.key(seed)
    ks = jax.random.split(key, 16)
    f32 = jnp.float32
    nrm = lambda k, shape, s: jax.random.normal(k, shape, f32) * s
    return {
        "x": nrm(ks[0], (BATCH, SEQ, D_MODEL), 1.0),
        "norm_g": 1.0 + nrm(ks[1], (DEPTH, 3, D_MODEL), 0.02),
        "ffn_w_gu": nrm(ks[2], (DEPTH, 2, D_MODEL, 2 * D_FF), D_MODEL ** -0.5),
        "ffn_w_down": nrm(ks[3], (DEPTH, 2, D_FF, D_MODEL), D_FF ** -0.5),
        "gla_w_in": nrm(ks[4], (N_A_LAYERS, D_MODEL, GLA_IN), D_MODEL ** -0.5),
        "gla_w_gate2": nrm(ks[5], (N_A_LAYERS, 2, GLA_GATE_RANK, GLA_QK), GLA_GATE_RANK ** -0.5),
        "gla_b_gate": nrm(ks[6], (N_A_LAYERS, 2, GLA_QK), 0.1),
        "gla_o_norm_g": 1.0 + nrm(ks[7], (N_A_LAYERS, GLA_DV), 0.02),
        "gla_w_out": nrm(ks[8], (N_A_LAYERS, GLA_V, D_MODEL), GLA_V ** -0.5),
        "diff_w_in": nrm(ks[9], (N_B_LAYERS, D_MODEL, DIFF_IN), D_MODEL ** -0.5),
        "diff_qk_norm_g": 1.0 + nrm(ks[10], (N_B_LAYERS, 2, DIFF_DH), 0.02),
        "diff_lambda": nrm(ks[11], (N_B_LAYERS, 4, DIFF_DH), 0.1),
        "diff_subln_g": 1.0 + nrm(ks[12], (N_B_LAYERS, 2 * DIFF_DH), 0.02),
        "diff_w_out": nrm(ks[13], (N_B_LAYERS, DIFF_V, D_MODEL), DIFF_V ** -0.5),
        "rel_bias_table": nrm(ks[14], (REL_BUCKETS, DIFF_HEADS), 0.5),
    }


def reference(x, norm_g, ffn_w_gu, ffn_w_down, gla_w_in, gla_w_gate2, gla_b_gate, gla_o_norm_g,
              gla_w_out, diff_w_in, diff_qk_norm_g, diff_lambda, diff_subln_g, diff_w_out, rel_bias_table):
    for i in range(DEPTH):
        j = i // N_MIXERS
        x = x + 0.5 * swiglu_ffn(rms_norm(x, norm_g[i, 0]), ffn_w_gu[i, 0], ffn_w_down[i, 0])
        h = rms_norm(x, norm_g[i, 1])
        if i % N_MIXERS == 0:
            x = x + gla_mixer(h, gla_w_in[j], gla_w_gate2[j], gla_b_gate[j], gla_o_norm_g[j], gla_w_out[j])
        else:
            lam_init = 0.8 - 0.6 * math.exp(-0.3 * i)
            x = x + diff_mixer(h, diff_w_in[j], diff_qk_norm_g[j], diff_lambda[j], diff_subln_g[j],
                               diff_w_out[j], rel_bias_table, lam_init)
        x = x + 0.5 * swiglu_ffn(rms_norm(x, norm_g[i, 2]), ffn_w_gu[i, 1], ffn_w_down[i, 1])
    return x
```

```python
import functools
import math

import jax
import jax.numpy as jnp
from jax import lax
from jax.experimental import pallas as pl
from jax.experimental.pallas import tpu as pltpu

F32 = jnp.float32
BF16 = jnp.bfloat16

EPS = 1e-6
GLA_GATE_NORM = 16.0
GLA_CHUNK = 64
REL_BUCKETS = 32
REL_MAX_DIST = 128
LANES = 128
VMEM_LIMIT = 48 * 1024 * 1024


def _cparams(*sem):
    return pltpu.CompilerParams(dimension_semantics=sem, vmem_limit_bytes=VMEM_LIMIT)


def _rms_scale(x):
    return x * lax.rsqrt(jnp.mean(x * x, axis=-1, keepdims=True) + EPS)


def _ffn_kernel(x_ref, g_ref, wg_ref, wu_ref, wd_ref, o_ref, h_ref, acc_ref):
    j = pl.program_id(1)

    @pl.when(j == 0)
    def _():
        h_ref[...] = (_rms_scale(x_ref[...]) * g_ref[...]).astype(BF16)

    h = h_ref[...]
    gate = jnp.dot(h, wg_ref[...], preferred_element_type=F32)
    up = jnp.dot(h, wu_ref[...], preferred_element_type=F32)
    act = (gate * jax.nn.sigmoid(gate) * up).astype(BF16)
    part = jnp.dot(act, wd_ref[...], preferred_element_type=F32)

    @pl.when(j == 0)
    def _():
        acc_ref[...] = part

    @pl.when(j > 0)
    def _():
        acc_ref[...] += part

    @pl.when(j == pl.num_programs(1) - 1)
    def _():
        o_ref[...] = x_ref[...] + 0.5 * acc_ref[...]


def _ffn(x2, g, w_gu, w_down, li, si, *, tm, tf):
    n, d = x2.shape
    f = w_down.shape[2]
    nf = f // tf
    return pl.pallas_call(
        _ffn_kernel,
        grid=(n // tm, nf),
        in_specs=[
            pl.BlockSpec((tm, d), lambda i, j: (i, 0)),
            pl.BlockSpec((1, d), lambda i, j: (0, 0)),
            pl.BlockSpec((None, None, d, tf), lambda i, j: (li, si, 0, j)),
            pl.BlockSpec((None, None, d, tf), lambda i, j: (li, si, 0, j + nf)),
            pl.BlockSpec((None, None, tf, d), lambda i, j: (li, si, j, 0)),
        ],
        out_specs=pl.BlockSpec((tm, d), lambda i, j: (i, 0)),
        out_shape=jax.ShapeDtypeStruct((n, d), F32),
        scratch_shapes=[pltpu.VMEM((tm, d), BF16), pltpu.VMEM((tm, d), F32)],
        compiler_params=_cparams("parallel", "arbitrary"),
        name="ffn",
    )(x2, g, w_gu, w_gu, w_down)


def _out_proj_kernel(x_ref, y_ref, w_ref, o_ref):
    o_ref[...] = x_ref[...] + jnp.dot(y_ref[...], w_ref[...], preferred_element_type=F32)


def _out_proj(x2, y2, w, li, *, tm):
    n, d = x2.shape
    kdim = y2.shape[1]
    return pl.pallas_call(
        _out_proj_kernel,
        grid=(n // tm,),
        in_specs=[
            pl.BlockSpec((tm, d), lambda i: (i, 0)),
            pl.BlockSpec((tm, kdim), lambda i: (i, 0)),
            pl.BlockSpec((None, kdim, d), lambda i: (li, 0, 0)),
        ],
        out_specs=pl.BlockSpec((tm, d), lambda i: (i, 0)),
        out_shape=jax.ShapeDtypeStruct((n, d), F32),
        compiler_params=_cparams("parallel"),
        name="out_proj",
    )(x2, y2, w)


def _log_sigmoid(z):
    return jnp.minimum(z, 0.0) - jnp.log(1.0 + jnp.exp(-jnp.abs(z)))


def _gla_proj_kernel(x_ref, g_ref, wm_ref, wlr_ref, w2_ref, bg_ref,
                     q_ref, k_ref, v_ref, go_ref, laf_ref, lab_ref, *, qk, vdim, rank, q_scale):
    h = (_rms_scale(x_ref[...]) * g_ref[...]).astype(BF16)
    dot = functools.partial(jnp.dot, preferred_element_type=F32)
    q_ref[...] = dot(h, wm_ref[:, 0:qk]) * q_scale
    k_ref[...] = dot(h, wm_ref[:, qk:2 * qk])
    v_ref[...] = dot(h, wm_ref[:, 2 * qk:2 * qk + vdim]).astype(BF16)
    go_ref[...] = dot(h, wm_ref[:, 2 * qk + vdim:2 * qk + 2 * vdim])
    lr = dot(h, wlr_ref[...])
    zf = dot(lr[:, 0:rank].astype(BF16), w2_ref[0]) + bg_ref[0:1, :]
    zb = dot(lr[:, rank:2 * rank].astype(BF16), w2_ref[1]) + bg_ref[1:2, :]
    laf_ref[...] = _log_sigmoid(zf) * (1.0 / GLA_GATE_NORM)
    lab_ref[...] = _log_sigmoid(zb) * (1.0 / GLA_GATE_NORM)


def _gla_proj(x2, g, w_main, w_lr, w2, bg, *, tm, heads):
    n, d = x2.shape
    rank, qk = w2.shape[1], w2.shape[2]
    vdim = (w_main.shape[1] - 2 * qk) // 2
    dk = qk // heads
    kern = functools.partial(_gla_proj_kernel, qk=qk, vdim=vdim, rank=rank, q_scale=dk ** -0.5)
    row = lambda i: (i, 0)
    const2 = lambda i: (0, 0)
    return pl.pallas_call(
        kern,
        grid=(n // tm,),
        in_specs=[
            pl.BlockSpec((tm, d), row),
            pl.BlockSpec((1, d), const2),
            pl.BlockSpec(w_main.shape, const2),
            pl.BlockSpec(w_lr.shape, const2),
            pl.BlockSpec(w2.shape, lambda i: (0, 0, 0)),
            pl.BlockSpec(bg.shape, const2),
        ],
        out_specs=[
            pl.BlockSpec((tm, qk), row), pl.BlockSpec((tm, qk), row),
            pl.BlockSpec((tm, vdim), row), pl.BlockSpec((tm, vdim), row),
            pl.BlockSpec((tm, qk), row), pl.BlockSpec((tm, qk), row),
        ],
        out_shape=[
            jax.ShapeDtypeStruct((n, qk), F32), jax.ShapeDtypeStruct((n, qk), F32),
            jax.ShapeDtypeStruct((n, vdim), BF16), jax.ShapeDtypeStruct((n, vdim), F32),
            jax.ShapeDtypeStruct((n, qk), F32), jax.ShapeDtypeStruct((n, qk), F32),
        ],
        compiler_params=_cparams("parallel"),
        name="gla_proj",
    )(x2, g, w_main, w_lr, w2, bg)


def _split3_dot(tri_bf, a):
    a0 = a.astype(BF16)
    r1 = a - a0.astype(F32)
    a1 = r1.astype(BF16)
    a2 = (r1 - a1.astype(F32)).astype(BF16)
    dot = functools.partial(jnp.dot, preferred_element_type=F32)
    return dot(tri_bf, a0) + dot(tri_bf, a1) + dot(tri_bf, a2)


def _gla_chunk(q, k, v_bf, la, st_ref, tri_bf, mask, end_row, q_scr, k_scr, b_scr):
    c = q.shape[0]
    b = _split3_dot(tri_bf, la)
    b_end = b[end_row:end_row + 1, :]
    st = st_ref[...]
    inter = lax.dot_general((q * jnp.exp(b)).astype(BF16), st.astype(BF16),
                            (((1,), (1,)), ((), ())), preferred_element_type=F32)

    q_scr[...] = q
    k_scr[...] = k
    b_scr[...] = b
    lane = lax.broadcasted_iota(jnp.int32, (c, c), 1)

    def col(j, a):
        bj = b_scr[pl.ds(j, 1), :]
        kj = k_scr[pl.ds(j, 1), :]
        t = q_scr[...] * jnp.exp(jnp.minimum(b_scr[...] - bj, 0.0)) * kj
        return jnp.where(lane == j, jnp.sum(t, axis=-1, keepdims=True), a)

    scores = lax.fori_loop(0, c, col, jnp.zeros((c, c), F32))
    scores = jnp.where(mask, scores, 0.0)
    intra = jnp.dot(scores.astype(BF16), v_bf, preferred_element_type=F32)

    k_dec = (k * jnp.exp(b_end - b)).astype(BF16)
    upd = lax.dot_general(v_bf, k_dec, (((0,), (0,)), ((), ())), preferred_element_type=F32)
    st_ref[...] = st * jnp.exp(b_end) + upd
    return inter + intra


def _gla_core_kernel(q_ref, k_ref, v_ref, g_ref, laf_ref, lab_ref, ong_ref, o_ref,
                     of_scr, ob_scr, sf_scr, sb_scr, q_scr, k_scr, b_scr, *, chunk):
    s = q_ref.shape[1]
    n = s // chunk
    row = lax.broadcasted_iota(jnp.int32, (chunk, chunk), 0)
    colm = lax.broadcasted_iota(jnp.int32, (chunk, chunk), 1)
    lower_bf = (colm <= row).astype(BF16)
    upper_bf = (colm >= row).astype(BF16)
    mask_f = colm <= row
    mask_b = colm > row
    sf_scr[...] = jnp.zeros_like(sf_scr)
    sb_scr[...] = jnp.zeros_like(sb_scr)

    def body(c, carry):
        rf = pl.multiple_of(c * chunk, chunk)
        of_scr[pl.ds(rf, chunk), :] = _gla_chunk(
            q_ref[0, pl.ds(rf, chunk), :], k_ref[0, pl.ds(rf, chunk), :], v_ref[0, pl.ds(rf, chunk), :],
            laf_ref[0, pl.ds(rf, chunk), :], sf_scr, lower_bf, mask_f, chunk - 1, q_scr, k_scr, b_scr)
        rb = pl.multiple_of((n - 1 - c) * chunk, chunk)
        ob_scr[pl.ds(rb, chunk), :] = _gla_chunk(
            q_ref[0, pl.ds(rb, chunk), :], k_ref[0, pl.ds(rb, chunk), :], v_ref[0, pl.ds(rb, chunk), :],
            lab_ref[0, pl.ds(rb, chunk), :], sb_scr, upper_bf, mask_b, 0, q_scr, k_scr, b_scr)
        return carry

    lax.fori_loop(0, n, body, 0)

    def fin(c, carry):
        r = pl.multiple_of(c * chunk, chunk)
        o = of_scr[pl.ds(r, chunk), :] + ob_scr[pl.ds(r, chunk), :]
        g = g_ref[0, pl.ds(r, chunk), :]
        y = _rms_scale(o) * ong_ref[...] * (g * jax.nn.sigmoid(g))
        o_ref[0, pl.ds(r, chunk), :] = y.astype(o_ref.dtype)
        return carry

    lax.fori_loop(0, n, fin, 0)


def _gla_core(q, k, v, go, laf, lab, ong, *, heads):
    bsz, s, qk = q.shape
    vdim = v.shape[2]
    dk, dv = qk // heads, vdim // heads
    blk = lambda w: pl.BlockSpec((1, s, w), lambda b, h: (b, 0, h))
    kern = functools.partial(_gla_core_kernel, chunk=GLA_CHUNK)
    return pl.pallas_call(
        kern,
        grid=(bsz, heads),
        in_specs=[blk(dk), blk(dk), blk(dv), blk(dv), blk(dk), blk(dk),
                  pl.BlockSpec((1, dv), lambda b, h: (0, 0))],
        out_specs=blk(dv),
        out_shape=jax.ShapeDtypeStruct((bsz, s, vdim), BF16),
        scratch_shapes=[
            pltpu.VMEM((s, dv), F32), pltpu.VMEM((s, dv), F32),
            pltpu.VMEM((dv, dk), F32), pltpu.VMEM((dv, dk), F32),
            pltpu.VMEM((GLA_CHUNK, dk), F32), pltpu.VMEM((GLA_CHUNK, dk), F32),
            pltpu.VMEM((GLA_CHUNK, dk), F32),
        ],
        compiler_params=_cparams("parallel", "parallel"),
        name="gla_core",
    )(q, k, v, go, laf, lab, ong)


def _group_mean_sq(t, bd_ref):
    cols = t.shape[1]
    parts = []
    for c0 in range(0, cols, LANES):
        sq = t[:, c0:c0 + LANES]
        parts.append(jnp.dot((sq * sq).astype(BF16), bd_ref[...], preferred_element_type=F32))
    return jnp.concatenate(parts, axis=1)


def _diff_proj_kernel(x_ref, g_ref, w_ref, gq_ref, gk_ref, bd_ref, q_ref, k_ref, v_ref, *, qk, q_scale):
    h = (_rms_scale(x_ref[...]) * g_ref[...]).astype(BF16)
    dot = functools.partial(jnp.dot, preferred_element_type=F32)
    q = dot(h, w_ref[:, 0:qk])
    q_ref[...] = (q * lax.rsqrt(_group_mean_sq(q, bd_ref) + EPS) * gq_ref[...] * q_scale).astype(BF16)
    k = dot(h, w_ref[:, qk:2 * qk])
    k_ref[...] = (k * lax.rsqrt(_group_mean_sq(k, bd_ref) + EPS) * gk_ref[...]).astype(BF16)
    v_ref[...] = dot(h, w_ref[:, 2 * qk:]).astype(BF16)


def _diff_proj(x2, g, w, gq_row, gk_row, bd, *, tm, dh):
    n, d = x2.shape
    qk = gq_row.shape[1]
    vdim = w.shape[1] - 2 * qk
    kern = functools.partial(_diff_proj_kernel, qk=qk, q_scale=dh ** -0.5)
    row = lambda i: (i, 0)
    const2 = lambda i: (0, 0)
    return pl.pallas_call(
        kern,
        grid=(n // tm,),
        in_specs=[
            pl.BlockSpec((tm, d), row),
            pl.BlockSpec((1, d), const2),
            pl.BlockSpec(w.shape, const2),
            pl.BlockSpec((1, qk), const2),
            pl.BlockSpec((1, qk), const2),
            pl.BlockSpec(bd.shape, const2),
        ],
        out_specs=[pl.BlockSpec((tm, qk), row), pl.BlockSpec((tm, qk), row), pl.BlockSpec((tm, vdim), row)],
        out_shape=[jax.ShapeDtypeStruct((n, qk), BF16), jax.ShapeDtypeStruct((n, qk), BF16),
                   jax.ShapeDtypeStruct((n, vdim), BF16)],
        compiler_params=_cparams("parallel"),
        name="diff_proj",
    )(x2, g, w, gq_row, gk_row, bd)


def _bucket_thresholds():
    nb = REL_BUCKETS // 2
    max_exact = nb // 2
    m = nb - max_exact
    return [int(math.ceil(max_exact * (REL_MAX_DIST / max_exact) ** (kk / m) - 1e-9)) for kk in range(1, m)]


def _rel_bias_kernel(table_ref, o_ref, *, tr):
    heads, _, s = o_ref.shape
    nb = REL_BUCKETS // 2
    max_exact = nb // 2
    i0 = pl.program_id(0) * tr
    qpos = i0 + lax.broadcasted_iota(jnp.int32, (tr, s), 0)
    kpos = lax.broadcasted_iota(jnp.int32, (tr, s), 1)
    rel = kpos - qpos
    n = jnp.abs(rel)
    large = jnp.full((tr, s), max_exact, jnp.int32)
    for thr in _bucket_thresholds():
        large = large + (n >= thr).astype(jnp.int32)
    bucket = jnp.where(rel > 0, nb, 0) + jnp.where(n < max_exact, n, large)

    def per_bucket(u, carry):
        hit = bucket == u
        for hh in range(heads):
            o_ref[hh] = jnp.where(hit, table_ref[u, hh], o_ref[hh])
        return carry

    for hh in range(heads):
        o_ref[hh] = jnp.zeros((tr, s), F32)
    lax.fori_loop(0, REL_BUCKETS, per_bucket, 0)


def _rel_bias(table, s, *, tr=32):
    heads = table.shape[1]
    return pl.pallas_call(
        functools.partial(_rel_bias_kernel, tr=tr),
        grid=(s // tr,),
        in_specs=[pl.BlockSpec(memory_space=pltpu.SMEM)],
        out_specs=pl.BlockSpec((heads, tr, s), lambda i: (0, i, 0)),
        out_shape=jax.ShapeDtypeStruct((heads, s, s), F32),
        compiler_params=_cparams("parallel"),
        name="rel_bias",
    )(table)


def _diff_attn_kernel(q_ref, k_ref, v_ref, bias_ref, lam_ref, sg_ref, o_ref, vx_ref, *, dh, lam_init):
    tq = q_ref.shape[1]
    s = k_ref.shape[1]
    dv = v_ref.shape[2]
    q = q_ref[0]
    k = k_ref[0]
    lane = lax.broadcasted_iota(jnp.int32, (tq, 2 * dh), 1)
    zero = jnp.zeros_like(q)
    q1 = jnp.where(lane < dh, q, zero)
    q2 = jnp.where(lane >= dh, q, zero)
    bias = bias_ref[0]

    vx_ref[:, 0:dv] = v_ref[0]
    vx_ref[:, dv:2 * dv] = jnp.ones((s, dv), BF16)
    vx = vx_ref[...]

    def branch(qc):
        sc = lax.dot_general(qc, k, (((1,), (1,)), ((), ())), preferred_element_type=F32) + bias
        e = jnp.exp(sc - jnp.max(sc, axis=-1, keepdims=True)).astype(BF16)
        pv = jnp.dot(e, vx, preferred_element_type=F32)
        return pv[:, 0:dv] / pv[:, dv:2 * dv]

    lv = lam_ref[...]
    lam = (jnp.exp(jnp.sum(lv[0:1] * lv[1:2], axis=-1, keepdims=True))
           - jnp.exp(jnp.sum(lv[2:3] * lv[3:4], axis=-1, keepdims=True)) + lam_init)
    o = branch(q1) - lam * branch(q2)
    o_ref[0] = (_rms_scale(o) * sg_ref[...] * (1.0 - lam_init)).astype(o_ref.dtype)


def _diff_attn(q, k, v, bias, lam_vecs, subln_g, *, heads, tq, lam_init):
    bsz, s, qkdim = q.shape
    dh = qkdim // heads // 2
    dv = v.shape[2] // heads
    kern = functools.partial(_diff_attn_kernel, dh=dh, lam_init=lam_init)
    return pl.pallas_call(
        kern,
        grid=(heads, s // tq, bsz),
        in_specs=[
            pl.BlockSpec((1, tq, 2 * dh), lambda h, i, b: (b, i, h)),
            pl.BlockSpec((1, s, 2 * dh), lambda h, i, b: (b, 0, h)),
            pl.BlockSpec((1, s, dv), lambda h, i, b: (b, 0, h)),
            pl.BlockSpec((1, tq, s), lambda h, i, b: (h, i, 0)),
            pl.BlockSpec(lam_vecs.shape, lambda h, i, b: (0, 0)),
            pl.BlockSpec((1, dv), lambda h, i, b: (0, 0)),
        ],
        out_specs=pl.BlockSpec((1, tq, dv), lambda h, i, b: (b, i, h)),
        out_shape=jax.ShapeDtypeStruct((bsz, s, heads * dv), BF16),
        scratch_shapes=[pltpu.VMEM((s, 2 * dv), BF16)],
        compiler_params=_cparams("parallel", "parallel", "arbitrary"),
        name="diff_attn",
    )(q, k, v, bias, lam_vecs, subln_g)


def _pick_tile(n, pref):
    t = min(n, pref)
    while n % t:
        t //= 2
    return t


def kernel(x, norm_g, ffn_w_gu, ffn_w_down, gla_w_in, gla_w_gate2, gla_b_gate, gla_o_norm_g, gla_w_out,
           diff_w_in, diff_qk_norm_g, diff_lambda, diff_subln_g, diff_w_out, rel_bias_table):
    bsz, s, d = x.shape
    n = bsz * s
    depth = norm_g.shape[0]
    d_ff = ffn_w_down.shape[2]
    gla_qk = gla_w_gate2.shape[3]
    gla_dv = gla_o_norm_g.shape[1]
    gla_heads = gla_w_out.shape[1] // gla_dv
    dh = diff_qk_norm_g.shape[2]
    diff_heads = rel_bias_table.shape[1]
    diff_qk = diff_heads * 2 * dh

    tm = _pick_tile(n, 512)
    tf = d_ff // 2 if (d_ff // 2) % LANES == 0 else d_ff
    tq = _pick_tile(s, 256)

    w_gu = ffn_w_gu.astype(BF16)
    w_down = ffn_w_down.astype(BF16)
    gla_main = gla_w_in[:, :, :gla_w_in.shape[2] - 2 * gla_w_gate2.shape[2]].astype(BF16)
    gla_lr = gla_w_in[:, :, gla_w_in.shape[2] - 2 * gla_w_gate2.shape[2]:].astype(BF16)
    gla_w2 = gla_w_gate2.astype(BF16)
    gla_wo = gla_w_out.astype(BF16)
    diff_wi = diff_w_in.astype(BF16)
    diff_wo = diff_w_out.astype(BF16)

    grp = jnp.arange(LANES) // dh
    bd = jnp.where(grp[:, None] == grp[None, :], 1.0 / dh, 0.0).astype(BF16)

    bias = _rel_bias(rel_bias_table.astype(F32), s) if depth > 1 else None

    x2 = x.reshape(n, d)
    for i in range(depth):
        j = i // 2
        x2 = _ffn(x2, norm_g[i, 0:1], w_gu, w_down, i, 0, tm=tm, tf=tf)
        g_mix = norm_g[i, 1:2]
        if i % 2 == 0:
            q, k, v, go, laf, lab = _gla_proj(x2, g_mix, gla_main[j], gla_lr[j], gla_w2[j], gla_b_gate[j],
                                              tm=tm, heads=gla_heads)
            r3 = lambda t: t.reshape(bsz, s, t.shape[1])
            y = _gla_core(r3(q), r3(k), r3(v), r3(go), r3(laf), r3(lab), gla_o_norm_g[j:j + 1],
                          heads=gla_heads)
            x2 = _out_proj(x2, y.reshape(n, -1), gla_wo, j, tm=tm)
        else:
            lam_init = 0.8 - 0.6 * math.exp(-0.3 * i)
            gq = jnp.tile(diff_qk_norm_g[j, 0], diff_qk // dh)[None, :]
            gk = jnp.tile(diff_qk_norm_g[j, 1], diff_qk // dh)[None, :]
            q, k, v = _diff_proj(x2, g_mix, diff_wi[j], gq, gk, bd, tm=tm, dh=dh)
            r3 = lambda t: t.reshape(bsz, s, t.shape[1])
            y = _diff_attn(r3(q), r3(k), r3(v), bias, diff_lambda[j], diff_subln_g[j:j + 1],
                           heads=diff_heads, tq=tq, lam_init=lam_init)
            x2 = _out_proj(x2, y.reshape(n, -1), diff_wo, j, tm=tm)
        x2 = _ffn(x2, norm_g[i, 2:3], w_gu, w_down, i, 1, tm=tm, tf=tf)
    return x2.reshape(bsz, s, d)
```

```python
import functools
import math

import jax
import jax.numpy as jnp
from jax import lax
from jax.experimental import pallas as pl
from jax.experimental.pallas import tpu as pltpu

F32 = jnp.float32
BF16 = jnp.bfloat16

EPS = 1e-6
GLA_GATE_NORM = 16.0
GLA_CHUNK = 64
GLA_HALF_RANGE = 40.0
REL_BUCKETS = 32
REL_MAX_DIST = 128
LANES = 128
VMEM_LIMIT = 48 * 1024 * 1024


def _cparams(*sem):
    return pltpu.CompilerParams(dimension_semantics=sem, vmem_limit_bytes=VMEM_LIMIT)


def _rms_scale(x):
    return x * lax.rsqrt(jnp.mean(x * x, axis=-1, keepdims=True) + EPS)


def _ffn_kernel(x_ref, g_ref, wg_ref, wu_ref, wd_ref, o_ref, h_ref, acc_ref):
    j = pl.program_id(1)

    @pl.when(j == 0)
    def _():
        h_ref[...] = (_rms_scale(x_ref[...]) * g_ref[...]).astype(BF16)

    h = h_ref[...]
    gate = jnp.dot(h, wg_ref[...], preferred_element_type=F32)
    up = jnp.dot(h, wu_ref[...], preferred_element_type=F32)
    act = (gate * jax.nn.sigmoid(gate) * up).astype(BF16)
    part = jnp.dot(act, wd_ref[...], preferred_element_type=F32)

    @pl.when(j == 0)
    def _():
        acc_ref[...] = part

    @pl.when(j > 0)
    def _():
        acc_ref[...] += part

    @pl.when(j == pl.num_programs(1) - 1)
    def _():
        o_ref[...] = x_ref[...] + 0.5 * acc_ref[...]


def _ffn(x2, g, w_gu, w_down, li, si, *, tm, tf):
    n, d = x2.shape
    f = w_down.shape[2]
    nf = f // tf
    return pl.pallas_call(
        _ffn_kernel,
        grid=(n // tm, nf),
        in_specs=[
            pl.BlockSpec((tm, d), lambda i, j: (i, 0)),
            pl.BlockSpec((1, d), lambda i, j: (0, 0)),
            pl.BlockSpec((None, None, d, tf), lambda i, j: (li, si, 0, j)),
            pl.BlockSpec((None, None, d, tf), lambda i, j: (li, si, 0, j + nf)),
            pl.BlockSpec((None, None, tf, d), lambda i, j: (li, si, j, 0)),
        ],
        out_specs=pl.BlockSpec((tm, d), lambda i, j: (i, 0)),
        out_shape=jax.ShapeDtypeStruct((n, d), F32),
        scratch_shapes=[pltpu.VMEM((tm, d), BF16), pltpu.VMEM((tm, d), F32)],
        compiler_params=_cparams("parallel", "arbitrary"),
        name="ffn",
    )(x2, g, w_gu, w_gu, w_down)


def _out_proj_kernel(x_ref, y_ref, w_ref, o_ref):
    o_ref[...] = x_ref[...] + jnp.dot(y_ref[...], w_ref[...], preferred_element_type=F32)


def _out_proj(x2, y2, w, li, *, tm):
    n, d = x2.shape
    kdim = y2.shape[1]
    return pl.pallas_call(
        _out_proj_kernel,
        grid=(n // tm,),
        in_specs=[
            pl.BlockSpec((tm, d), lambda i: (i, 0)),
            pl.BlockSpec((tm, kdim), lambda i: (i, 0)),
            pl.BlockSpec((None, kdim, d), lambda i: (li, 0, 0)),
        ],
        out_specs=pl.BlockSpec((tm, d), lambda i: (i, 0)),
        out_shape=jax.ShapeDtypeStruct((n, d), F32),
        compiler_params=_cparams("parallel"),
        name="out_proj",
    )(x2, y2, w)


def _log_sigmoid(z):
    return jnp.minimum(z, 0.0) - jnp.log(1.0 + jnp.exp(-jnp.abs(z)))


def _gla_proj_kernel(x_ref, g_ref, wm_ref, wlr_ref, w2_ref, bg_ref,
                     q_ref, k_ref, v_ref, go_ref, laf_ref, lab_ref, *, qk, vdim, rank, q_scale):
    h = (_rms_scale(x_ref[...]) * g_ref[...]).astype(BF16)
    dot = functools.partial(jnp.dot, preferred_element_type=F32)
    q_ref[...] = dot(h, wm_ref[:, 0:qk]) * q_scale
    k_ref[...] = dot(h, wm_ref[:, qk:2 * qk])
    v_ref[...] = dot(h, wm_ref[:, 2 * qk:2 * qk + vdim]).astype(BF16)
    go_ref[...] = dot(h, wm_ref[:, 2 * qk + vdim:2 * qk + 2 * vdim])
    lr = dot(h, wlr_ref[...])
    zf = dot(lr[:, 0:rank].astype(BF16), w2_ref[0]) + bg_ref[0:1, :]
    zb = dot(lr[:, rank:2 * rank].astype(BF16), w2_ref[1]) + bg_ref[1:2, :]
    laf_ref[...] = _log_sigmoid(zf) * (1.0 / GLA_GATE_NORM)
    lab_ref[...] = _log_sigmoid(zb) * (1.0 / GLA_GATE_NORM)


def _gla_proj(x2, g, w_main, w_lr, w2, bg, *, tm, heads):
    n, d = x2.shape
    rank, qk = w2.shape[1], w2.shape[2]
    vdim = (w_main.shape[1] - 2 * qk) // 2
    dk = qk // heads
    kern = functools.partial(_gla_proj_kernel, qk=qk, vdim=vdim, rank=rank, q_scale=dk ** -0.5)
    row = lambda i: (i, 0)
    const2 = lambda i: (0, 0)
    return pl.pallas_call(
        kern,
        grid=(n // tm,),
        in_specs=[
            pl.BlockSpec((tm, d), row),
            pl.BlockSpec((1, d), const2),
            pl.BlockSpec(w_main.shape, const2),
            pl.BlockSpec(w_lr.shape, const2),
            pl.BlockSpec(w2.shape, lambda i: (0, 0, 0)),
            pl.BlockSpec(bg.shape, const2),
        ],
        out_specs=[
            pl.BlockSpec((tm, qk), row), pl.BlockSpec((tm, qk), row),
            pl.BlockSpec((tm, vdim), row), pl.BlockSpec((tm, vdim), row),
            pl.BlockSpec((tm, qk), row), pl.BlockSpec((tm, qk), row),
        ],
        out_shape=[
            jax.ShapeDtypeStruct((n, qk), F32), jax.ShapeDtypeStruct((n, qk), F32),
            jax.ShapeDtypeStruct((n, vdim), BF16), jax.ShapeDtypeStruct((n, vdim), F32),
            jax.ShapeDtypeStruct((n, qk), F32), jax.ShapeDtypeStruct((n, qk), F32),
        ],
        compiler_params=_cparams("parallel"),
        name="gla_proj",
    )(x2, g, w_main, w_lr, w2, bg)


def _split3_dot(tri_bf, a):
    a0 = a.astype(BF16)
    r1 = a - a0.astype(F32)
    a1 = r1.astype(BF16)
    a2 = (r1 - a1.astype(F32)).astype(BF16)
    dot = functools.partial(jnp.dot, preferred_element_type=F32)
    return dot(tri_bf, a0) + dot(tri_bf, a1) + dot(tri_bf, a2)


def _gla_chunk(q, k, v_bf, la, st_ref, tri_bf, mask, end_row, q_scr, k_scr, b_scr, factorised):
    c = q.shape[0]
    contract_last = (((1,), (1,)), ((), ()))
    b = _split3_dot(tri_bf, la)
    b_end = b[end_row:end_row + 1, :]
    st = st_ref[...]

    if factorised:
        b_first = b[c - 1 - end_row:c - end_row, :]
        mid = 0.5 * (b_first + b_end)
        qf = q * jnp.exp(b - mid)
        kf = k * jnp.exp(mid - b)
        q_dec = (qf * jnp.exp(mid)).astype(BF16)
        k_dec = (kf * jnp.exp(b_end - mid)).astype(BF16)
        scores = lax.dot_general(qf.astype(BF16), kf.astype(BF16), contract_last,
                                 preferred_element_type=F32)
    else:
        q_dec = (q * jnp.exp(b)).astype(BF16)
        k_dec = (k * jnp.exp(b_end - b)).astype(BF16)
        q_scr[...] = q
        k_scr[...] = k
        b_scr[...] = b
        lane = lax.broadcasted_iota(jnp.int32, (c, c), 1)

        def col(j, a):
            bj = b_scr[pl.ds(j, 1), :]
            kj = k_scr[pl.ds(j, 1), :]
            t = q_scr[...] * jnp.exp(jnp.minimum(b_scr[...] - bj, 0.0)) * kj
            return jnp.where(lane == j, jnp.sum(t, axis=-1, keepdims=True), a)

        scores = lax.fori_loop(0, c, col, jnp.zeros((c, c), F32))

    scores = jnp.where(mask, scores, 0.0)
    intra = jnp.dot(scores.astype(BF16), v_bf, preferred_element_type=F32)
    inter = lax.dot_general(q_dec, st.astype(BF16), contract_last, preferred_element_type=F32)
    upd = lax.dot_general(v_bf, k_dec, (((0,), (0,)), ((), ())), preferred_element_type=F32)
    st_ref[...] = st * jnp.exp(b_end) + upd
    return inter + intra


def _gla_core_kernel(q_ref, k_ref, v_ref, g_ref, laf_ref, lab_ref, ong_ref, o_ref,
                     of_scr, ob_scr, sf_scr, sb_scr, q_scr, k_scr, b_scr, *, chunk):
    s = q_ref.shape[1]
    n = s // chunk
    row = lax.broadcasted_iota(jnp.int32, (chunk, chunk), 0)
    colm = lax.broadcasted_iota(jnp.int32, (chunk, chunk), 1)
    lower_bf = (colm <= row).astype(BF16)
    upper_bf = (colm >= row).astype(BF16)
    mask_f = colm <= row
    mask_b = colm > row
    sf_scr[...] = jnp.zeros_like(sf_scr)
    sb_scr[...] = jnp.zeros_like(sb_scr)

    def scan(factorised):
        def body(c, carry):
            rf = pl.multiple_of(c * chunk, chunk)
            of_scr[pl.ds(rf, chunk), :] = _gla_chunk(
                q_ref[0, pl.ds(rf, chunk), :], k_ref[0, pl.ds(rf, chunk), :], v_ref[0, pl.ds(rf, chunk), :],
                laf_ref[0, pl.ds(rf, chunk), :], sf_scr, lower_bf, mask_f, chunk - 1,
                q_scr, k_scr, b_scr, factorised)
            rb = pl.multiple_of((n - 1 - c) * chunk, chunk)
            ob_scr[pl.ds(rb, chunk), :] = _gla_chunk(
                q_ref[0, pl.ds(rb, chunk), :], k_ref[0, pl.ds(rb, chunk), :], v_ref[0, pl.ds(rb, chunk), :],
                lab_ref[0, pl.ds(rb, chunk), :], sb_scr, upper_bf, mask_b, 0,
                q_scr, k_scr, b_scr, factorised)
            return carry

        lax.fori_loop(0, n, body, 0)

    la_min = jnp.minimum(jnp.min(laf_ref[0]), jnp.min(lab_ref[0]))
    in_range = la_min * (0.5 * (chunk - 1)) >= -GLA_HALF_RANGE

    @pl.when(in_range)
    def _():
        scan(True)

    @pl.when(jnp.logical_not(in_range))
    def _():
        scan(False)

    def fin(c, carry):
        r = pl.multiple_of(c * chunk, chunk)
        o = of_scr[pl.ds(r, chunk), :] + ob_scr[pl.ds(r, chunk), :]
        g = g_ref[0, pl.ds(r, chunk), :]
        y = _rms_scale(o) * ong_ref[...] * (g * jax.nn.sigmoid(g))
        o_ref[0, pl.ds(r, chunk), :] = y.astype(o_ref.dtype)
        return carry

    lax.fori_loop(0, n, fin, 0)


def _gla_core(q, k, v, go, laf, lab, ong, *, heads):
    bsz, s, qk = q.shape
    vdim = v.shape[2]
    dk, dv = qk // heads, vdim // heads
    blk = lambda w: pl.BlockSpec((1, s, w), lambda b, h: (b, 0, h))
    kern = functools.partial(_gla_core_kernel, chunk=GLA_CHUNK)
    return pl.pallas_call(
        kern,
        grid=(bsz, heads),
        in_specs=[blk(dk), blk(dk), blk(dv), blk(dv), blk(dk), blk(dk),
                  pl.BlockSpec((1, dv), lambda b, h: (0, 0))],
        out_specs=blk(dv),
        out_shape=jax.ShapeDtypeStruct((bsz, s, vdim), BF16),
        scratch_shapes=[
            pltpu.VMEM((s, dv), F32), pltpu.VMEM((s, dv), F32),
            pltpu.VMEM((dv, dk), F32), pltpu.VMEM((dv, dk), F32),
            pltpu.VMEM((GLA_CHUNK, dk), F32), pltpu.VMEM((GLA_CHUNK, dk), F32),
            pltpu.VMEM((GLA_CHUNK, dk), F32),
        ],
        compiler_params=_cparams("parallel", "parallel"),
        name="gla_core",
    )(q, k, v, go, laf, lab, ong)


def _group_mean_sq(t, bd_ref):
    cols = t.shape[1]
    parts = []
    for c0 in range(0, cols, LANES):
        sq = t[:, c0:c0 + LANES]
        parts.append(jnp.dot((sq * sq).astype(BF16), bd_ref[...], preferred_element_type=F32))
    return jnp.concatenate(parts, axis=1)


def _diff_proj_kernel(x_ref, g_ref, w_ref, gq_ref, gk_ref, bd_ref, q_ref, k_ref, v_ref, *, qk, q_scale):
    h = (_rms_scale(x_ref[...]) * g_ref[...]).astype(BF16)
    dot = functools.partial(jnp.dot, preferred_element_type=F32)
    q = dot(h, w_ref[:, 0:qk])
    q_ref[...] = (q * lax.rsqrt(_group_mean_sq(q, bd_ref) + EPS) * gq_ref[...] * q_scale).astype(BF16)
    k = dot(h, w_ref[:, qk:2 * qk])
    k_ref[...] = (k * lax.rsqrt(_group_mean_sq(k, bd_ref) + EPS) * gk_ref[...]).astype(BF16)
    v_ref[...] = dot(h, w_ref[:, 2 * qk:]).astype(BF16)


def _diff_proj(x2, g, w, gq_row, gk_row, bd, *, tm, dh):
    n, d = x2.shape
    qk = gq_row.shape[1]
    vdim = w.shape[1] - 2 * qk
    kern = functools.partial(_diff_proj_kernel, qk=qk, q_scale=dh ** -0.5)
    row = lambda i: (i, 0)
    const2 = lambda i: (0, 0)
    return pl.pallas_call(
        kern,
        grid=(n // tm,),
        in_specs=[
            pl.BlockSpec((tm, d), row),
            pl.BlockSpec((1, d), const2),
            pl.BlockSpec(w.shape, const2),
            pl.BlockSpec((1, qk), const2),
            pl.BlockSpec((1, qk), const2),
            pl.BlockSpec(bd.shape, const2),
        ],
        out_specs=[pl.BlockSpec((tm, qk), row), pl.BlockSpec((tm, qk), row), pl.BlockSpec((tm, vdim), row)],
        out_shape=[jax.ShapeDtypeStruct((n, qk), BF16), jax.ShapeDtypeStruct((n, qk), BF16),
                   jax.ShapeDtypeStruct((n, vdim), BF16)],
        compiler_params=_cparams("parallel"),
        name="diff_proj",
    )(x2, g, w, gq_row, gk_row, bd)


def _bucket_thresholds():
    nb = REL_BUCKETS // 2
    max_exact = nb // 2
    m = nb - max_exact
    return [int(math.ceil(max_exact * (REL_MAX_DIST / max_exact) ** (kk / m) - 1e-9)) for kk in range(1, m)]


def _rel_bias_kernel(table_ref, o_ref, *, tr):
    heads, _, s = o_ref.shape
    nb = REL_BUCKETS // 2
    max_exact = nb // 2
    i0 = pl.program_id(0) * tr
    qpos = i0 + lax.broadcasted_iota(jnp.int32, (tr, s), 0)
    kpos = lax.broadcasted_iota(jnp.int32, (tr, s), 1)
    rel = kpos - qpos
    n = jnp.abs(rel)
    large = jnp.full((tr, s), max_exact, jnp.int32)
    for thr in _bucket_thresholds():
        large = large + (n >= thr).astype(jnp.int32)
    bucket = jnp.where(rel > 0, nb, 0) + jnp.where(n < max_exact, n, large)

    def per_bucket(u, carry):
        hit = bucket == u
        for hh in range(heads):
            o_ref[hh] = jnp.where(hit, table_ref[u, hh], o_ref[hh])
        return carry

    for hh in range(heads):
        o_ref[hh] = jnp.zeros((tr, s), F32)
    lax.fori_loop(0, REL_BUCKETS, per_bucket, 0)


def _rel_bias(table, s, *, tr=32):
    heads = table.shape[1]
    return pl.pallas_call(
        functools.partial(_rel_bias_kernel, tr=tr),
        grid=(s // tr,),
        in_specs=[pl.BlockSpec(memory_space=pltpu.SMEM)],
        out_specs=pl.BlockSpec((heads, tr, s), lambda i: (0, i, 0)),
        out_shape=jax.ShapeDtypeStruct((heads, s, s), F32),
        compiler_params=_cparams("parallel"),
        name="rel_bias",
    )(table)


def _diff_attn_kernel(q_ref, k_ref, v_ref, bias_ref, lam_ref, sg_ref, o_ref, vx_ref, *, dh, lam_init):
    tq = q_ref.shape[1]
    s = k_ref.shape[1]
    dv = v_ref.shape[2]
    q = q_ref[0]
    k = k_ref[0]
    lane = lax.broadcasted_iota(jnp.int32, (tq, 2 * dh), 1)
    zero = jnp.zeros_like(q)
    q1 = jnp.where(lane < dh, q, zero)
    q2 = jnp.where(lane >= dh, q, zero)
    bias = bias_ref[0]

    vx_ref[:, 0:dv] = v_ref[0]
    vx_ref[:, dv:2 * dv] = jnp.ones((s, dv), BF16)
    vx = vx_ref[...]

    def branch(qc):
        sc = lax.dot_general(qc, k, (((1,), (1,)), ((), ())), preferred_element_type=F32) + bias
        e = jnp.exp(sc - jnp.max(sc, axis=-1, keepdims=True)).astype(BF16)
        pv = jnp.dot(e, vx, preferred_element_type=F32)
        return pv[:, 0:dv] / pv[:, dv:2 * dv]

    lv = lam_ref[...]
    lam = (jnp.exp(jnp.sum(lv[0:1] * lv[1:2], axis=-1, keepdims=True))
           - jnp.exp(jnp.sum(lv[2:3] * lv[3:4], axis=-1, keepdims=True)) + lam_init)
    o = branch(q1) - lam * branch(q2)
    o_ref[0] = (_rms_scale(o) * sg_ref[...] * (1.0 - lam_init)).astype(o_ref.dtype)


def _diff_attn(q, k, v, bias, lam_vecs, subln_g, *, heads, tq, lam_init):
    bsz, s, qkdim = q.shape
    dh = qkdim // heads // 2
    dv = v.shape[2] // heads
    kern = functools.partial(_diff_attn_kernel, dh=dh, lam_init=lam_init)
    return pl.pallas_call(
        kern,
        grid=(heads, s // tq, bsz),
        in_specs=[
            pl.BlockSpec((1, tq, 2 * dh), lambda h, i, b: (b, i, h)),
            pl.BlockSpec((1, s, 2 * dh), lambda h, i, b: (b, 0, h)),
            pl.BlockSpec((1, s, dv), lambda h, i, b: (b, 0, h)),
            pl.BlockSpec((1, tq, s), lambda h, i, b: (h, i, 0)),
            pl.BlockSpec(lam_vecs.shape, lambda h, i, b: (0, 0)),
            pl.BlockSpec((1, dv), lambda h, i, b: (0, 0)),
        ],
        out_specs=pl.BlockSpec((1, tq, dv), lambda h, i, b: (b, i, h)),
        out_shape=jax.ShapeDtypeStruct((bsz, s, heads * dv), BF16),
        scratch_shapes=[pltpu.VMEM((s, 2 * dv), BF16)],
        compiler_params=_cparams("parallel", "parallel", "arbitrary"),
        name="diff_attn",
    )(q, k, v, bias, lam_vecs, subln_g)


def _pick_tile(n, pref):
    t = min(n, pref)
    while n % t:
        t //= 2
    return t


def kernel(x, norm_g, ffn_w_gu, ffn_w_down, gla_w_in, gla_w_gate2, gla_b_gate, gla_o_norm_g, gla_w_out,
           diff_w_in, diff_qk_norm_g, diff_lambda, diff_subln_g, diff_w_out, rel_bias_table):
    bsz, s, d = x.shape
    n = bsz * s
    depth = norm_g.shape[0]
    d_ff = ffn_w_down.shape[2]
    gla_qk = gla_w_gate2.shape[3]
    gla_dv = gla_o_norm_g.shape[1]
    gla_heads = gla_w_out.shape[1] // gla_dv
    dh = diff_qk_norm_g.shape[2]
    diff_heads = rel_bias_table.shape[1]
    diff_qk = diff_heads * 2 * dh

    tm = _pick_tile(n, 512)
    tf = d_ff // 2 if (d_ff // 2) % LANES == 0 else d_ff
    tq = _pick_tile(s, 256)

    w_gu = ffn_w_gu.astype(BF16)
    w_down = ffn_w_down.astype(BF16)
    gla_main = gla_w_in[:, :, :gla_w_in.shape[2] - 2 * gla_w_gate2.shape[2]].astype(BF16)
    gla_lr = gla_w_in[:, :, gla_w_in.shape[2] - 2 * gla_w_gate2.shape[2]:].astype(BF16)
    gla_w2 = gla_w_gate2.astype(BF16)
    gla_wo = gla_w_out.astype(BF16)
    diff_wi = diff_w_in.astype(BF16)
    diff_wo = diff_w_out.astype(BF16)

    grp = jnp.arange(LANES) // dh
    bd = jnp.where(grp[:, None] == grp[None, :], 1.0 / dh, 0.0).astype(BF16)

    bias = _rel_bias(rel_bias_table.astype(F32), s) if depth > 1 else None

    x2 = x.reshape(n, d)
    for i in range(depth):
        j = i // 2
        x2 = _ffn(x2, norm_g[i, 0:1], w_gu, w_down, i, 0, tm=tm, tf=tf)
        g_mix = norm_g[i, 1:2]
        if i % 2 == 0:
            q, k, v, go, laf, lab = _gla_proj(x2, g_mix, gla_main[j], gla_lr[j], gla_w2[j], gla_b_gate[j],
                                              tm=tm, heads=gla_heads)
            r3 = lambda t: t.reshape(bsz, s, t.shape[1])
            y = _gla_core(r3(q), r3(k), r3(v), r3(go), r3(laf), r3(lab), gla_o_norm_g[j:j + 1],
                          heads=gla_heads)
            x2 = _out_proj(x2, y.reshape(n, -1), gla_wo, j, tm=tm)
        else:
            lam_init = 0.8 - 0.6 * math.exp(-0.3 * i)
            gq = jnp.tile(diff_qk_norm_g[j, 0], diff_qk // dh)[None, :]
            gk = jnp.tile(diff_qk_norm_g[j, 1], diff_qk // dh)[None, :]
            q, k, v = _diff_proj(x2, g_mix, diff_wi[j], gq, gk, bd, tm=tm, dh=dh)
            r3 = lambda t: t.reshape(bsz, s, t.shape[1])
            y = _diff_attn(r3(q), r3(k), r3(v), bias, diff_lambda[j], diff_subln_g[j:j + 1],
                           heads=diff_heads, tq=tq, lam_init=lam_init)
            x2 = _out_proj(x2, y.reshape(n, -1), diff_wo, j, tm=tm)
        x2 = _ffn(x2, norm_g[i, 2:3], w_gu, w_down, i, 1, tm=tm, tf=tf)
    return x2.reshape(bsz, s, d)
```

```python
import functools
import math

import jax
import jax.numpy as jnp
from jax import lax
from jax.experimental import pallas as pl
from jax.experimental.pallas import tpu as pltpu

F32 = jnp.float32
BF16 = jnp.bfloat16

EPS = 1e-6
GLA_GATE_NORM = 16.0
GLA_CHUNK = 128
GLA_HALF_RANGE = 60.0
DIFF_MAX_GAP = 60.0
REL_BUCKETS = 32
REL_MAX_DIST = 128
LANES = 128
VMEM_LIMIT = 48 * 1024 * 1024


def _cparams(*sem):
    return pltpu.CompilerParams(dimension_semantics=sem, vmem_limit_bytes=VMEM_LIMIT)


def _rms_scale(x):
    return x * lax.rsqrt(jnp.mean(x * x, axis=-1, keepdims=True) + EPS)


def _resident(block_shape, index_map):
    return pl.BlockSpec(block_shape, index_map, pipeline_mode=pl.Buffered(1))


def _ffn_body(x, g_ref, wgu_ref, wd_ref, o_ref):
    f = wd_ref.shape[0]
    h = (_rms_scale(x) * g_ref[...]).astype(BF16)
    gate = jnp.dot(h, wgu_ref[:, 0:f], preferred_element_type=F32)
    up = jnp.dot(h, wgu_ref[:, f:2 * f], preferred_element_type=F32)
    act = (gate * jax.nn.sigmoid(gate) * up).astype(BF16)
    o_ref[...] = x + 0.5 * jnp.dot(act, wd_ref[...], preferred_element_type=F32)


def _ffn_kernel(x_ref, g_ref, wgu_ref, wd_ref, o_ref):
    _ffn_body(x_ref[...], g_ref, wgu_ref, wd_ref, o_ref)


def _mix_ffn_kernel(x_ref, y_ref, wo_ref, g_ref, wgu_ref, wd_ref, o_ref):
    x = x_ref[...] + jnp.dot(y_ref[...], wo_ref[...], preferred_element_type=F32)
    _ffn_body(x, g_ref, wgu_ref, wd_ref, o_ref)


def _ffn(x2, g, w_gu, w_down, li, si, *, tm, mix=None):
    n, d = x2.shape
    f = w_down.shape[2]
    row = lambda i: (i, 0)
    ffn_specs = [
        _resident((1, d), lambda i: (0, 0)),
        _resident((None, None, d, 2 * f), lambda i: (li, si, 0, 0)),
        _resident((None, None, f, d), lambda i: (li, si, 0, 0)),
    ]
    if mix is None:
        kern, operands, specs = _ffn_kernel, (x2,), [pl.BlockSpec((tm, d), row)]
    else:
        y2, w_out, lj = mix
        kdim = y2.shape[1]
        kern, operands = _mix_ffn_kernel, (x2, y2, w_out)
        specs = [pl.BlockSpec((tm, d), row), pl.BlockSpec((tm, kdim), row),
                 _resident((None, kdim, d), lambda i: (lj, 0, 0))]
    return pl.pallas_call(
        kern,
        grid=(n // tm,),
        in_specs=specs + ffn_specs,
        out_specs=pl.BlockSpec((tm, d), row),
        out_shape=jax.ShapeDtypeStruct((n, d), F32),
        compiler_params=_cparams("parallel"),
        name="ffn" if mix is None else "mix_ffn",
    )(*operands, g, w_gu, w_down)


def _log_sigmoid(z):
    return jnp.minimum(z, 0.0) - jnp.log(1.0 + jnp.exp(-jnp.abs(z)))


def _gla_proj_kernel(x_ref, g_ref, wm_ref, wlr_ref, w2_ref, bg_ref,
                     q_ref, k_ref, v_ref, go_ref, laf_ref, lab_ref, *, qk, vdim, rank, q_scale):
    h = (_rms_scale(x_ref[...]) * g_ref[...]).astype(BF16)
    dot = functools.partial(jnp.dot, preferred_element_type=F32)
    q_ref[...] = dot(h, wm_ref[:, 0:qk]) * q_scale
    k_ref[...] = dot(h, wm_ref[:, qk:2 * qk])
    v_ref[...] = dot(h, wm_ref[:, 2 * qk:2 * qk + vdim]).astype(BF16)
    go_ref[...] = dot(h, wm_ref[:, 2 * qk + vdim:2 * qk + 2 * vdim])
    lr = dot(h, wlr_ref[...])
    zf = dot(lr[:, 0:rank].astype(BF16), w2_ref[0]) + bg_ref[0:1, :]
    zb = dot(lr[:, rank:2 * rank].astype(BF16), w2_ref[1]) + bg_ref[1:2, :]
    laf_ref[...] = _log_sigmoid(zf) * (1.0 / GLA_GATE_NORM)
    lab_ref[...] = _log_sigmoid(zb) * (1.0 / GLA_GATE_NORM)


def _gla_proj(x2, g, w_main, w_lr, w2, bg, *, tm, heads):
    n, d = x2.shape
    rank, qk = w2.shape[1], w2.shape[2]
    vdim = (w_main.shape[1] - 2 * qk) // 2
    dk = qk // heads
    kern = functools.partial(_gla_proj_kernel, qk=qk, vdim=vdim, rank=rank, q_scale=dk ** -0.5)
    row = lambda i: (i, 0)
    const2 = lambda i: (0, 0)
    return pl.pallas_call(
        kern,
        grid=(n // tm,),
        in_specs=[
            pl.BlockSpec((tm, d), row),
            pl.BlockSpec((1, d), const2),
            pl.BlockSpec(w_main.shape, const2),
            pl.BlockSpec(w_lr.shape, const2),
            pl.BlockSpec(w2.shape, lambda i: (0, 0, 0)),
            pl.BlockSpec(bg.shape, const2),
        ],
        out_specs=[
            pl.BlockSpec((tm, qk), row), pl.BlockSpec((tm, qk), row),
            pl.BlockSpec((tm, vdim), row), pl.BlockSpec((tm, vdim), row),
            pl.BlockSpec((tm, qk), row), pl.BlockSpec((tm, qk), row),
        ],
        out_shape=[
            jax.ShapeDtypeStruct((n, qk), F32), jax.ShapeDtypeStruct((n, qk), F32),
            jax.ShapeDtypeStruct((n, vdim), BF16), jax.ShapeDtypeStruct((n, vdim), F32),
            jax.ShapeDtypeStruct((n, qk), F32), jax.ShapeDtypeStruct((n, qk), F32),
        ],
        compiler_params=_cparams("parallel"),
        name="gla_proj",
    )(x2, g, w_main, w_lr, w2, bg)


def _cumsum_rows(x, reverse):
    c = x.shape[0]
    row = lax.broadcasted_iota(jnp.int32, x.shape, 0)
    step = 1
    while step < c:
        if reverse:
            x = x + jnp.where(row < c - step, pltpu.roll(x, c - step, axis=0), 0.0)
        else:
            x = x + jnp.where(row >= step, pltpu.roll(x, step, axis=0), 0.0)
        step *= 2
    return x


def _gla_chunk(q, k, v_bf, la, st, mask, reverse, scr, factorised):
    c = q.shape[0]
    contract_last = (((1,), (1,)), ((), ()))
    b = _cumsum_rows(la, reverse)
    end_row = 0 if reverse else c - 1
    b_end = b[end_row:end_row + 1, :]

    if factorised:
        b_first = b[c - 1 - end_row:c - end_row, :]
        mid = 0.5 * (b_first + b_end)
        qf = q * jnp.exp(b - mid)
        kf = k * jnp.exp(mid - b)
        q_dec = (qf * jnp.exp(mid)).astype(BF16)
        k_dec = (kf * jnp.exp(b_end - mid)).astype(BF16)
        scores = lax.dot_general(qf.astype(BF16), kf.astype(BF16), contract_last,
                                 preferred_element_type=F32)
    else:
        q_dec = (q * jnp.exp(b)).astype(BF16)
        k_dec = (k * jnp.exp(b_end - b)).astype(BF16)
        q_scr, k_scr, b_scr = scr
        q_scr[...] = q
        k_scr[...] = k
        b_scr[...] = b
        lane = lax.broadcasted_iota(jnp.int32, (c, c), 1)

        def col(j, a):
            bj = b_scr[pl.ds(j, 1), :]
            kj = k_scr[pl.ds(j, 1), :]
            t = q_scr[...] * jnp.exp(jnp.minimum(b_scr[...] - bj, 0.0)) * kj
            return jnp.where(lane == j, jnp.sum(t, axis=-1, keepdims=True), a)

        scores = lax.fori_loop(0, c, col, jnp.zeros((c, c), F32))

    scores = jnp.where(mask, scores, 0.0)
    intra = jnp.dot(scores.astype(BF16), v_bf, preferred_element_type=F32)
    inter = lax.dot_general(q_dec, st.astype(BF16), contract_last, preferred_element_type=F32)
    upd = lax.dot_general(v_bf, k_dec, (((0,), (0,)), ((), ())), preferred_element_type=F32)
    return inter + intra, st * jnp.exp(b_end) + upd


def _gla_core_kernel(q_ref, k_ref, v_ref, g_ref, laf_ref, lab_ref, ong_ref, o_ref,
                     of_scr, ob_scr, sf_scr, sb_scr, q_scr, k_scr, b_scr, *, chunk):
    s = q_ref.shape[1]
    n = s // chunk
    row = lax.broadcasted_iota(jnp.int32, (chunk, chunk), 0)
    colm = lax.broadcasted_iota(jnp.int32, (chunk, chunk), 1)
    mask_f = colm <= row
    mask_b = colm > row
    sf_scr[...] = jnp.zeros_like(sf_scr)
    sb_scr[...] = jnp.zeros_like(sb_scr)
    scr = (q_scr, k_scr, b_scr)

    def scan(factorised):
        def body(c, carry):
            rf = pl.ds(pl.multiple_of(c * chunk, chunk), chunk)
            rb = pl.ds(pl.multiple_of((n - 1 - c) * chunk, chunk), chunk)
            args_f = (q_ref[0, rf, :], k_ref[0, rf, :], v_ref[0, rf, :], laf_ref[0, rf, :], sf_scr[...])
            args_b = (q_ref[0, rb, :], k_ref[0, rb, :], v_ref[0, rb, :], lab_ref[0, rb, :], sb_scr[...])
            out_f, st_f = _gla_chunk(*args_f, mask_f, False, scr, factorised)
            out_b, st_b = _gla_chunk(*args_b, mask_b, True, scr, factorised)
            of_scr[rf, :] = out_f
            ob_scr[rb, :] = out_b
            sf_scr[...] = st_f
            sb_scr[...] = st_b
            return carry

        lax.fori_loop(0, n, body, 0)

    la_min = jnp.minimum(jnp.min(laf_ref[0]), jnp.min(lab_ref[0]))
    in_range = la_min * (0.5 * (chunk - 1)) >= -GLA_HALF_RANGE

    @pl.when(in_range)
    def _():
        scan(True)

    @pl.when(jnp.logical_not(in_range))
    def _():
        scan(False)

    def fin(c, carry):
        r = pl.ds(pl.multiple_of(c * chunk, chunk), chunk)
        o = of_scr[r, :] + ob_scr[r, :]
        g = g_ref[0, r, :]
        y = _rms_scale(o) * ong_ref[...] * (g * jax.nn.sigmoid(g))
        o_ref[0, r, :] = y.astype(o_ref.dtype)
        return carry

    lax.fori_loop(0, n, fin, 0)


def _gla_core(q, k, v, go, laf, lab, ong, *, heads):
    bsz, s, qk = q.shape
    vdim = v.shape[2]
    dk, dv = qk // heads, vdim // heads
    chunk = min(GLA_CHUNK, s)
    blk = lambda w: pl.BlockSpec((1, s, w), lambda b, h: (b, 0, h))
    kern = functools.partial(_gla_core_kernel, chunk=chunk)
    return pl.pallas_call(
        kern,
        grid=(bsz, heads),
        in_specs=[blk(dk), blk(dk), blk(dv), blk(dv), blk(dk), blk(dk),
                  pl.BlockSpec((1, dv), lambda b, h: (0, 0))],
        out_specs=blk(dv),
        out_shape=jax.ShapeDtypeStruct((bsz, s, vdim), BF16),
        scratch_shapes=[
            pltpu.VMEM((s, dv), F32), pltpu.VMEM((s, dv), F32),
            pltpu.VMEM((dv, dk), F32), pltpu.VMEM((dv, dk), F32),
            pltpu.VMEM((chunk, dk), F32), pltpu.VMEM((chunk, dk), F32),
            pltpu.VMEM((chunk, dk), F32),
        ],
        compiler_params=_cparams("parallel", "parallel"),
        name="gla_core",
    )(q, k, v, go, laf, lab, ong)


def _group_mean_sq(t, bd_ref):
    cols = t.shape[1]
    parts = []
    for c0 in range(0, cols, LANES):
        sq = t[:, c0:c0 + LANES]
        parts.append(jnp.dot((sq * sq).astype(BF16), bd_ref[...], preferred_element_type=F32))
    return jnp.concatenate(parts, axis=1)


def _diff_proj_kernel(x_ref, g_ref, w_ref, gq_ref, gk_ref, bd_ref, q_ref, k_ref, v_ref, *, qk, q_scale):
    h = (_rms_scale(x_ref[...]) * g_ref[...]).astype(BF16)
    dot = functools.partial(jnp.dot, preferred_element_type=F32)
    q = dot(h, w_ref[:, 0:qk])
    q_ref[...] = (q * lax.rsqrt(_group_mean_sq(q, bd_ref) + EPS) * gq_ref[...] * q_scale).astype(BF16)
    k = dot(h, w_ref[:, qk:2 * qk])
    k_ref[...] = (k * lax.rsqrt(_group_mean_sq(k, bd_ref) + EPS) * gk_ref[...]).astype(BF16)
    v_ref[...] = dot(h, w_ref[:, 2 * qk:]).astype(BF16)


def _diff_proj(x2, g, w, gq_row, gk_row, bd, *, tm, dh):
    n, d = x2.shape
    qk = gq_row.shape[1]
    vdim = w.shape[1] - 2 * qk
    kern = functools.partial(_diff_proj_kernel, qk=qk, q_scale=dh ** -0.5)
    row = lambda i: (i, 0)
    const2 = lambda i: (0, 0)
    return pl.pallas_call(
        kern,
        grid=(n // tm,),
        in_specs=[
            pl.BlockSpec((tm, d), row),
            pl.BlockSpec((1, d), const2),
            pl.BlockSpec(w.shape, const2),
            pl.BlockSpec((1, qk), const2),
            pl.BlockSpec((1, qk), const2),
            pl.BlockSpec(bd.shape, const2),
        ],
        out_specs=[pl.BlockSpec((tm, qk), row), pl.BlockSpec((tm, qk), row), pl.BlockSpec((tm, vdim), row)],
        out_shape=[jax.ShapeDtypeStruct((n, qk), BF16), jax.ShapeDtypeStruct((n, qk), BF16),
                   jax.ShapeDtypeStruct((n, vdim), BF16)],
        compiler_params=_cparams("parallel"),
        name="diff_proj",
    )(x2, g, w, gq_row, gk_row, bd)


def _bucket_thresholds():
    nb = REL_BUCKETS // 2
    max_exact = nb // 2
    m = nb - max_exact
    return [int(math.ceil(max_exact * (REL_MAX_DIST / max_exact) ** (kk / m) - 1e-9)) for kk in range(1, m)]


def _rel_bias_kernel(table_ref, off_ref, o_ref, prof_ref, *, tr):
    heads, _, s = o_ref.shape
    nb = REL_BUCKETS // 2
    max_exact = nb // 2
    i0 = pl.program_id(0) * tr

    @pl.when(pl.program_id(0) == 0)
    def _():
        rel = lax.broadcasted_iota(jnp.int32, (1, 2 * s), 1) - s
        n = jnp.abs(rel)
        large = jnp.full((1, 2 * s), max_exact, jnp.int32)
        for thr in _bucket_thresholds():
            large = large + (n >= thr).astype(jnp.int32)
        bucket = jnp.where(rel > 0, nb, 0) + jnp.where(n < max_exact, n, large)
        for hh in range(heads):
            acc = jnp.zeros((1, 2 * s), F32)
            for u in range(REL_BUCKETS):
                acc = jnp.where(bucket == u, table_ref[u, hh], acc)
            prof_ref[hh:hh + 1, :] = acc - off_ref[hh]

    base = pl.multiple_of(s - tr - i0, tr)
    for hh in range(heads):
        w = prof_ref[hh:hh + 1, pl.ds(base, s + tr)]
        x = pltpu.roll(jnp.broadcast_to(w, (tr, s + tr)), 0, axis=1, stride=1, stride_axis=0)
        o_ref[hh] = x[:, tr:tr + s]


def _rel_bias(table, off, s, *, tr=LANES):
    heads = table.shape[1]
    return pl.pallas_call(
        functools.partial(_rel_bias_kernel, tr=tr),
        grid=(s // tr,),
        in_specs=[pl.BlockSpec(memory_space=pltpu.SMEM), pl.BlockSpec(memory_space=pltpu.SMEM)],
        out_specs=pl.BlockSpec((heads, tr, s), lambda i: (0, i, 0)),
        out_shape=jax.ShapeDtypeStruct((heads, s, s), F32),
        scratch_shapes=[pltpu.VMEM((heads, 2 * s), F32)],
        compiler_params=_cparams("arbitrary"),
        name="rel_bias",
    )(table, off)


def _diff_attn_kernel(flag_ref, q_ref, k_ref, v_ref, bias_ref, lam_ref, sg_ref, o_ref, vx_ref, *, dh, lam_init):
    tq = q_ref.shape[1]
    s = k_ref.shape[1]
    dv = v_ref.shape[2]
    q = q_ref[0]
    k = k_ref[0]
    lane = lax.broadcasted_iota(jnp.int32, (tq, 2 * dh), 1)
    zero = jnp.zeros_like(q)
    qs = jnp.concatenate([jnp.where(lane < dh, q, zero), jnp.where(lane >= dh, q, zero)], axis=0)

    vx_ref[:, 0:dv] = v_ref[0]
    vx_ref[:, dv:2 * dv] = jnp.ones((s, dv), BF16)

    lv = lam_ref[...]
    lam = (jnp.exp(jnp.sum(lv[0:1] * lv[1:2], axis=-1, keepdims=True))
           - jnp.exp(jnp.sum(lv[2:3] * lv[3:4], axis=-1, keepdims=True)) + lam_init)

    def attend(exact_max):
        sc = lax.dot_general(qs, k, (((1,), (1,)), ((), ())), preferred_element_type=F32)
        x = sc.reshape(2, tq, s) + bias_ref[...]
        if exact_max:
            x = x - jnp.max(x, axis=-1, keepdims=True)
        e = jnp.exp(x).astype(BF16).reshape(2 * tq, s)
        pv = jnp.dot(e, vx_ref[...], preferred_element_type=F32)
        r = pv[:, 0:dv] / pv[:, dv:2 * dv]
        o = r[0:tq] - lam * r[tq:2 * tq]
        o_ref[0] = (_rms_scale(o) * sg_ref[...] * (1.0 - lam_init)).astype(o_ref.dtype)

    @pl.when(flag_ref[0] == 1)
    def _():
        attend(False)

    @pl.when(flag_ref[0] != 1)
    def _():
        attend(True)


def _diff_attn(flag, q, k, v, bias, lam_vecs, subln_g, *, heads, tq, lam_init):
    bsz, s, qkdim = q.shape
    dh = qkdim // heads // 2
    dv = v.shape[2] // heads
    kern = functools.partial(_diff_attn_kernel, dh=dh, lam_init=lam_init)
    return pl.pallas_call(
        kern,
        grid=(heads, s // tq, bsz),
        in_specs=[
            pl.BlockSpec(memory_space=pltpu.SMEM),
            pl.BlockSpec((1, tq, 2 * dh), lambda h, i, b: (b, i, h)),
            pl.BlockSpec((1, s, 2 * dh), lambda h, i, b: (b, 0, h)),
            pl.BlockSpec((1, s, dv), lambda h, i, b: (b, 0, h)),
            pl.BlockSpec((1, tq, s), lambda h, i, b: (h, i, 0)),
            pl.BlockSpec(lam_vecs.shape, lambda h, i, b: (0, 0)),
            pl.BlockSpec((1, dv), lambda h, i, b: (0, 0)),
        ],
        out_specs=pl.BlockSpec((1, tq, dv), lambda h, i, b: (b, i, h)),
        out_shape=jax.ShapeDtypeStruct((bsz, s, heads * dv), BF16),
        scratch_shapes=[pltpu.VMEM((s, 2 * dv), BF16)],
        compiler_params=_cparams("parallel", "parallel", "arbitrary"),
        name="diff_attn",
    )(flag, q, k, v, bias, lam_vecs, subln_g)


def _pick_tile(n, pref):
    t = min(n, pref)
    while n % t:
        t //= 2
    return t


def kernel(x, norm_g, ffn_w_gu, ffn_w_down, gla_w_in, gla_w_gate2, gla_b_gate, gla_o_norm_g, gla_w_out,
           diff_w_in, diff_qk_norm_g, diff_lambda, diff_subln_g, diff_w_out, rel_bias_table):
    bsz, s, d = x.shape
    n = bsz * s
    depth = norm_g.shape[0]
    gla_dv = gla_o_norm_g.shape[1]
    gla_heads = gla_w_out.shape[1] // gla_dv
    dh = diff_qk_norm_g.shape[2]
    diff_heads = rel_bias_table.shape[1]
    diff_qk = diff_heads * 2 * dh

    tm = _pick_tile(n, 512)
    tq = _pick_tile(s, 512)

    w_gu = ffn_w_gu.astype(BF16)
    w_down = ffn_w_down.astype(BF16)
    gla_main = gla_w_in[:, :, :gla_w_in.shape[2] - 2 * gla_w_gate2.shape[2]].astype(BF16)
    gla_lr = gla_w_in[:, :, gla_w_in.shape[2] - 2 * gla_w_gate2.shape[2]:].astype(BF16)
    gla_w2 = gla_w_gate2.astype(BF16)
    gla_wo = gla_w_out.astype(BF16)
    diff_wi = diff_w_in.astype(BF16)
    diff_wo = diff_w_out.astype(BF16)

    grp = jnp.arange(LANES) // dh
    bd = jnp.where(grp[:, None] == grp[None, :], 1.0 / dh, 0.0).astype(BF16)

    table = rel_bias_table.astype(F32)
    g_abs = jnp.max(jnp.abs(diff_qk_norm_g.astype(F32)), axis=-1)
    s_max = jnp.max(g_abs[:, 0] * g_abs[:, 1]) * (dh ** 0.5) * (1.0 + 2.0 ** -6)
    t_max, t_min = jnp.max(table, axis=0), jnp.min(table, axis=0)
    offset_ok = 2.0 * s_max + jnp.max(t_max - t_min) <= DIFF_MAX_GAP
    bias_off = jnp.where(offset_ok, s_max + t_max, 0.0)
    attn_flag = offset_ok.astype(jnp.int32).reshape(1)
    bias = _rel_bias(table, bias_off, s) if depth > 1 else None

    x2 = x.reshape(n, d)
    for i in range(depth):
        j = i // 2
        x2 = _ffn(x2, norm_g[i, 0:1], w_gu, w_down, i, 0, tm=tm)
        g_mix = norm_g[i, 1:2]
        r3 = lambda t: t.reshape(bsz, s, t.shape[1])
        if i % 2 == 0:
            q, k, v, go, laf, lab = _gla_proj(x2, g_mix, gla_main[j], gla_lr[j], gla_w2[j], gla_b_gate[j],
                                              tm=tm, heads=gla_heads)
            y = _gla_core(r3(q), r3(k), r3(v), r3(go), r3(laf), r3(lab), gla_o_norm_g[j:j + 1],
                          heads=gla_heads)
            mix = (y.reshape(n, -1), gla_wo, j)
        else:
            lam_init = 0.8 - 0.6 * math.exp(-0.3 * i)
            gq = jnp.tile(diff_qk_norm_g[j, 0], diff_qk // dh)[None, :]
            gk = jnp.tile(diff_qk_norm_g[j, 1], diff_qk // dh)[None, :]
            q, k, v = _diff_proj(x2, g_mix, diff_wi[j], gq, gk, bd, tm=tm, dh=dh)
            y = _diff_attn(attn_flag, r3(q), r3(k), r3(v), bias, diff_lambda[j], diff_subln_g[j:j + 1],
                           heads=diff_heads, tq=tq, lam_init=lam_init)
            mix = (y.reshape(n, -1), diff_wo, j)
        x2 = _ffn(x2, norm_g[i, 2:3], w_gu, w_down, i, 1, tm=tm, mix=mix)
    return x2.reshape(bsz, s, d)
```

```python
import functools
import math

import jax
import jax.numpy as jnp
from jax import lax
from jax.experimental import pallas as pl
from jax.experimental.pallas import tpu as pltpu

F32 = jnp.float32
BF16 = jnp.bfloat16

EPS = 1e-6
GLA_GATE_NORM = 16.0
GLA_CHUNK = 128
GLA_HALF_RANGE = 60.0
DIFF_MAX_GAP = 60.0
REL_BUCKETS = 32
REL_MAX_DIST = 128
LANES = 128
VMEM_LIMIT = 48 * 1024 * 1024


def _cparams(*sem):
    return pltpu.CompilerParams(dimension_semantics=sem, vmem_limit_bytes=VMEM_LIMIT)


def _rms_scale(x):
    return x * lax.rsqrt(jnp.mean(x * x, axis=-1, keepdims=True) + EPS)


def _resident(block_shape, index_map):
    return pl.BlockSpec(block_shape, index_map, pipeline_mode=pl.Buffered(1))


def _ffn_body(x, g_ref, wg_ref, wu_ref, wd_ref, o_ref):
    h = (_rms_scale(x) * g_ref[...]).astype(BF16)
    gate = jnp.dot(h, wg_ref[...], preferred_element_type=F32)
    up = jnp.dot(h, wu_ref[...], preferred_element_type=F32)
    act = (gate * jax.nn.sigmoid(gate) * up).astype(BF16)
    o_ref[...] = x + 0.5 * jnp.dot(act, wd_ref[...], preferred_element_type=F32)


def _ffn_kernel(x_ref, g_ref, wg_ref, wu_ref, wd_ref, o_ref):
    _ffn_body(x_ref[...], g_ref, wg_ref, wu_ref, wd_ref, o_ref)


def _mix_ffn_kernel(x_ref, y_ref, wo_ref, g_ref, wg_ref, wu_ref, wd_ref, o_ref):
    x = x_ref[...] + jnp.dot(y_ref[...], wo_ref[...], preferred_element_type=F32)
    _ffn_body(x, g_ref, wg_ref, wu_ref, wd_ref, o_ref)


def _ffn(x2, g, w_gu, w_down, li, si, *, tm, mix=None):
    n, d = x2.shape
    f = w_down.shape[2]
    row = lambda i: (i, 0)
    ffn_specs = [
        _resident((1, d), lambda i: (0, 0)),
        _resident((None, None, d, f), lambda i: (li, si, 0, 0)),
        _resident((None, None, d, f), lambda i: (li, si, 0, 1)),
        _resident((None, None, f, d), lambda i: (li, si, 0, 0)),
    ]
    if mix is None:
        kern, operands, specs = _ffn_kernel, (x2,), [pl.BlockSpec((tm, d), row)]
    else:
        y2, w_out, lj = mix
        kdim = y2.shape[1]
        kern, operands = _mix_ffn_kernel, (x2, y2, w_out)
        specs = [pl.BlockSpec((tm, d), row), pl.BlockSpec((tm, kdim), row),
                 _resident((None, kdim, d), lambda i: (lj, 0, 0))]
    return pl.pallas_call(
        kern,
        grid=(n // tm,),
        in_specs=specs + ffn_specs,
        out_specs=pl.BlockSpec((tm, d), row),
        out_shape=jax.ShapeDtypeStruct((n, d), F32),
        compiler_params=_cparams("parallel"),
        name="ffn" if mix is None else "mix_ffn",
    )(*operands, g, w_gu, w_gu, w_down)


def _log_sigmoid(z):
    return jnp.minimum(z, 0.0) - jnp.log(1.0 + jnp.exp(-jnp.abs(z)))


def _gla_proj_kernel(x_ref, g_ref, wm_ref, wlr_ref, w2_ref, bg_ref,
                     q_ref, k_ref, v_ref, go_ref, laf_ref, lab_ref, *, qk, vdim, rank, q_scale):
    h = (_rms_scale(x_ref[...]) * g_ref[...]).astype(BF16)
    dot = functools.partial(jnp.dot, preferred_element_type=F32)
    q_ref[...] = dot(h, wm_ref[:, 0:qk]) * q_scale
    k_ref[...] = dot(h, wm_ref[:, qk:2 * qk])
    v_ref[...] = dot(h, wm_ref[:, 2 * qk:2 * qk + vdim]).astype(BF16)
    go_ref[...] = dot(h, wm_ref[:, 2 * qk + vdim:2 * qk + 2 * vdim])
    lr = dot(h, wlr_ref[...])
    zf = dot(lr[:, 0:rank].astype(BF16), w2_ref[0]) + bg_ref[0:1, :]
    zb = dot(lr[:, rank:2 * rank].astype(BF16), w2_ref[1]) + bg_ref[1:2, :]
    laf_ref[...] = _log_sigmoid(zf) * (1.0 / GLA_GATE_NORM)
    lab_ref[...] = _log_sigmoid(zb) * (1.0 / GLA_GATE_NORM)


def _gla_proj(x2, g, w_main, w_lr, w2, bg, *, tm, heads):
    n, d = x2.shape
    rank, qk = w2.shape[1], w2.shape[2]
    vdim = (w_main.shape[1] - 2 * qk) // 2
    dk = qk // heads
    kern = functools.partial(_gla_proj_kernel, qk=qk, vdim=vdim, rank=rank, q_scale=dk ** -0.5)
    row = lambda i: (i, 0)
    const2 = lambda i: (0, 0)
    return pl.pallas_call(
        kern,
        grid=(n // tm,),
        in_specs=[
            pl.BlockSpec((tm, d), row),
            pl.BlockSpec((1, d), const2),
            pl.BlockSpec(w_main.shape, const2),
            pl.BlockSpec(w_lr.shape, const2),
            pl.BlockSpec(w2.shape, lambda i: (0, 0, 0)),
            pl.BlockSpec(bg.shape, const2),
        ],
        out_specs=[
            pl.BlockSpec((tm, qk), row), pl.BlockSpec((tm, qk), row),
            pl.BlockSpec((tm, vdim), row), pl.BlockSpec((tm, vdim), row),
            pl.BlockSpec((tm, qk), row), pl.BlockSpec((tm, qk), row),
        ],
        out_shape=[
            jax.ShapeDtypeStruct((n, qk), F32), jax.ShapeDtypeStruct((n, qk), F32),
            jax.ShapeDtypeStruct((n, vdim), BF16), jax.ShapeDtypeStruct((n, vdim), F32),
            jax.ShapeDtypeStruct((n, qk), F32), jax.ShapeDtypeStruct((n, qk), F32),
        ],
        compiler_params=_cparams("parallel"),
        name="gla_proj",
    )(x2, g, w_main, w_lr, w2, bg)


def _cumsum_rows(x, reverse):
    c, w = x.shape
    sub = 8
    t = c // sub
    x3 = x.reshape(t, sub, w)
    row = lax.broadcasted_iota(jnp.int32, (t, sub, w), 1)
    step = 1
    while step < sub:
        if reverse:
            x3 = x3 + jnp.where(row < sub - step, pltpu.roll(x3, sub - step, axis=1), 0.0)
        else:
            x3 = x3 + jnp.where(row >= step, pltpu.roll(x3, step, axis=1), 0.0)
        step *= 2
    order = range(t - 1, -1, -1) if reverse else range(t)
    edge = 0 if reverse else sub - 1
    tiles, carry = [None] * t, None
    for i in order:
        tile = x3[i] if carry is None else x3[i] + carry
        tiles[i] = tile
        carry = tile[edge:edge + 1, :]
    return jnp.concatenate(tiles, axis=0)


def _gla_chunk(q, k, v_bf, la, st, mask, reverse, scr, factorised):
    c = q.shape[0]
    contract_last = (((1,), (1,)), ((), ()))
    b = _cumsum_rows(la, reverse)
    end_row = 0 if reverse else c - 1
    b_end = b[end_row:end_row + 1, :]

    if factorised:
        b_first = b[c - 1 - end_row:c - end_row, :]
        mid = 0.5 * (b_first + b_end)
        qf = q * jnp.exp(b - mid)
        kf = k * jnp.exp(mid - b)
        q_dec = (qf * jnp.exp(mid)).astype(BF16)
        k_dec = (kf * jnp.exp(b_end - mid)).astype(BF16)
        scores = lax.dot_general(qf.astype(BF16), kf.astype(BF16), contract_last,
                                 preferred_element_type=F32)
    else:
        q_dec = (q * jnp.exp(b)).astype(BF16)
        k_dec = (k * jnp.exp(b_end - b)).astype(BF16)
        q_scr, k_scr, b_scr = scr
        q_scr[...] = q
        k_scr[...] = k
        b_scr[...] = b
        lane = lax.broadcasted_iota(jnp.int32, (c, c), 1)

        def col(j, a):
            bj = b_scr[pl.ds(j, 1), :]
            kj = k_scr[pl.ds(j, 1), :]
            t = q_scr[...] * jnp.exp(jnp.minimum(b_scr[...] - bj, 0.0)) * kj
            return jnp.where(lane == j, jnp.sum(t, axis=-1, keepdims=True), a)

        scores = lax.fori_loop(0, c, col, jnp.zeros((c, c), F32))

    scores = jnp.where(mask, scores, 0.0)
    intra = jnp.dot(scores.astype(BF16), v_bf, preferred_element_type=F32)
    inter = lax.dot_general(q_dec, st.astype(BF16), contract_last, preferred_element_type=F32)
    upd = lax.dot_general(v_bf, k_dec, (((0,), (0,)), ((), ())), preferred_element_type=F32)
    return inter + intra, st * jnp.exp(b_end) + upd


def _gla_core_kernel(q_ref, k_ref, v_ref, g_ref, laf_ref, lab_ref, ong_ref, o_ref,
                     of_scr, ob_scr, sf_scr, sb_scr, q_scr, k_scr, b_scr, *, chunk):
    s = q_ref.shape[1]
    n = s // chunk
    row = lax.broadcasted_iota(jnp.int32, (chunk, chunk), 0)
    colm = lax.broadcasted_iota(jnp.int32, (chunk, chunk), 1)
    mask_f = colm <= row
    mask_b = colm > row
    sf_scr[...] = jnp.zeros_like(sf_scr)
    sb_scr[...] = jnp.zeros_like(sb_scr)
    scr = (q_scr, k_scr, b_scr)

    def scan(factorised):
        def body(c, carry):
            rf = pl.ds(pl.multiple_of(c * chunk, chunk), chunk)
            rb = pl.ds(pl.multiple_of((n - 1 - c) * chunk, chunk), chunk)
            args_f = (q_ref[0, rf, :], k_ref[0, rf, :], v_ref[0, rf, :], laf_ref[0, rf, :], sf_scr[...])
            args_b = (q_ref[0, rb, :], k_ref[0, rb, :], v_ref[0, rb, :], lab_ref[0, rb, :], sb_scr[...])
            out_f, st_f = _gla_chunk(*args_f, mask_f, False, scr, factorised)
            out_b, st_b = _gla_chunk(*args_b, mask_b, True, scr, factorised)
            of_scr[rf, :] = out_f
            ob_scr[rb, :] = out_b
            sf_scr[...] = st_f
            sb_scr[...] = st_b
            return carry

        lax.fori_loop(0, n, body, 0, unroll=4 if factorised and n % 4 == 0 else 1)

    la_min = jnp.minimum(jnp.min(laf_ref[0]), jnp.min(lab_ref[0]))
    in_range = la_min * (0.5 * (chunk - 1)) >= -GLA_HALF_RANGE

    @pl.when(in_range)
    def _():
        scan(True)

    @pl.when(jnp.logical_not(in_range))
    def _():
        scan(False)

    def fin(c, carry):
        r = pl.ds(pl.multiple_of(c * chunk, chunk), chunk)
        o = of_scr[r, :] + ob_scr[r, :]
        g = g_ref[0, r, :]
        y = _rms_scale(o) * ong_ref[...] * (g * jax.nn.sigmoid(g))
        o_ref[0, r, :] = y.astype(o_ref.dtype)
        return carry

    lax.fori_loop(0, n, fin, 0, unroll=2 if n % 2 == 0 else 1)


def _gla_core(q, k, v, go, laf, lab, ong, *, heads):
    bsz, s, qk = q.shape
    vdim = v.shape[2]
    dk, dv = qk // heads, vdim // heads
    chunk = min(GLA_CHUNK, s)
    blk = lambda w: pl.BlockSpec((1, s, w), lambda b, h: (b, 0, h))
    kern = functools.partial(_gla_core_kernel, chunk=chunk)
    return pl.pallas_call(
        kern,
        grid=(bsz, heads),
        in_specs=[blk(dk), blk(dk), blk(dv), blk(dv), blk(dk), blk(dk),
                  pl.BlockSpec((1, dv), lambda b, h: (0, 0))],
        out_specs=blk(dv),
        out_shape=jax.ShapeDtypeStruct((bsz, s, vdim), BF16),
        scratch_shapes=[
            pltpu.VMEM((s, dv), F32), pltpu.VMEM((s, dv), F32),
            pltpu.VMEM((dv, dk), F32), pltpu.VMEM((dv, dk), F32),
            pltpu.VMEM((chunk, dk), F32), pltpu.VMEM((chunk, dk), F32),
            pltpu.VMEM((chunk, dk), F32),
        ],
        compiler_params=_cparams("parallel", "parallel"),
        name="gla_core",
    )(q, k, v, go, laf, lab, ong)


def _group_mean_sq(t, bd_ref):
    cols = t.shape[1]
    parts = []
    for c0 in range(0, cols, LANES):
        sq = t[:, c0:c0 + LANES]
        parts.append(jnp.dot((sq * sq).astype(BF16), bd_ref[...], preferred_element_type=F32))
    return jnp.concatenate(parts, axis=1)


def _diff_proj_kernel(x_ref, g_ref, w_ref, gq_ref, gk_ref, bd_ref, q_ref, k_ref, vx_ref, *, qk, dv, q_scale):
    h = (_rms_scale(x_ref[...]) * g_ref[...]).astype(BF16)
    dot = functools.partial(jnp.dot, preferred_element_type=F32)
    q = dot(h, w_ref[:, 0:qk])
    q_ref[...] = (q * lax.rsqrt(_group_mean_sq(q, bd_ref) + EPS) * gq_ref[...] * q_scale).astype(BF16)
    k = dot(h, w_ref[:, qk:2 * qk])
    k_ref[...] = (k * lax.rsqrt(_group_mean_sq(k, bd_ref) + EPS) * gk_ref[...]).astype(BF16)
    v = dot(h, w_ref[:, 2 * qk:]).astype(BF16)
    ones = jnp.ones((v.shape[0], dv), BF16)
    for hh in range(v.shape[1] // dv):
        vx_ref[:, 2 * hh * dv:(2 * hh + 1) * dv] = v[:, hh * dv:(hh + 1) * dv]
        vx_ref[:, (2 * hh + 1) * dv:(2 * hh + 2) * dv] = ones


def _diff_proj(x2, g, w, gq_row, gk_row, bd, *, tm, dh, dv):
    n, d = x2.shape
    qk = gq_row.shape[1]
    vdim = 2 * (w.shape[1] - 2 * qk)
    kern = functools.partial(_diff_proj_kernel, qk=qk, dv=dv, q_scale=dh ** -0.5)
    row = lambda i: (i, 0)
    const2 = lambda i: (0, 0)
    return pl.pallas_call(
        kern,
        grid=(n // tm,),
        in_specs=[
            pl.BlockSpec((tm, d), row),
            pl.BlockSpec((1, d), const2),
            pl.BlockSpec(w.shape, const2),
            pl.BlockSpec((1, qk), const2),
            pl.BlockSpec((1, qk), const2),
            pl.BlockSpec(bd.shape, const2),
        ],
        out_specs=[pl.BlockSpec((tm, qk), row), pl.BlockSpec((tm, qk), row), pl.BlockSpec((tm, vdim), row)],
        out_shape=[jax.ShapeDtypeStruct((n, qk), BF16), jax.ShapeDtypeStruct((n, qk), BF16),
                   jax.ShapeDtypeStruct((n, vdim), BF16)],
        compiler_params=_cparams("parallel"),
        name="diff_proj",
    )(x2, g, w, gq_row, gk_row, bd)


def _bucket_thresholds():
    nb = REL_BUCKETS // 2
    max_exact = nb // 2
    m = nb - max_exact
    return [int(math.ceil(max_exact * (REL_MAX_DIST / max_exact) ** (kk / m) - 1e-9)) for kk in range(1, m)]


def _rel_bias_kernel(table_ref, off_ref, o_ref, prof_ref, *, tr):
    heads, _, s = o_ref.shape
    nb = REL_BUCKETS // 2
    max_exact = nb // 2
    i0 = pl.program_id(0) * tr

    @pl.when(pl.program_id(0) == 0)
    def _():
        rel = lax.broadcasted_iota(jnp.int32, (1, 2 * s), 1) - s
        n = jnp.abs(rel)
        large = jnp.full((1, 2 * s), max_exact, jnp.int32)
        for thr in _bucket_thresholds():
            large = large + (n >= thr).astype(jnp.int32)
        bucket = jnp.where(rel > 0, nb, 0) + jnp.where(n < max_exact, n, large)
        for hh in range(heads):
            acc = jnp.zeros((1, 2 * s), F32)
            for u in range(REL_BUCKETS):
                acc = jnp.where(bucket == u, table_ref[u, hh], acc)
            prof_ref[hh:hh + 1, :] = acc - off_ref[hh]

    base = pl.multiple_of(s - tr - i0, tr)
    for hh in range(heads):
        w = prof_ref[hh:hh + 1, pl.ds(base, s + tr)]
        x = pltpu.roll(jnp.broadcast_to(w, (tr, s + tr)), 0, axis=1, stride=1, stride_axis=0)
        o_ref[hh] = x[:, tr:tr + s]


def _rel_bias(table, off, s, *, tr=LANES):
    heads = table.shape[1]
    return pl.pallas_call(
        functools.partial(_rel_bias_kernel, tr=tr),
        grid=(s // tr,),
        in_specs=[pl.BlockSpec(memory_space=pltpu.SMEM), pl.BlockSpec(memory_space=pltpu.SMEM)],
        out_specs=pl.BlockSpec((heads, tr, s), lambda i: (0, i, 0)),
        out_shape=jax.ShapeDtypeStruct((heads, s, s), F32),
        scratch_shapes=[pltpu.VMEM((heads, 2 * s), F32)],
        compiler_params=_cparams("arbitrary"),
        name="rel_bias",
    )(table, off)


def _diff_attn_kernel(flag_ref, q_ref, k_ref, vx_ref, bias_ref, lam_ref, sg_ref, o_ref, *, dh, lam_init):
    nb, tq = q_ref.shape[0], q_ref.shape[1]
    s = k_ref.shape[1]
    dv = vx_ref.shape[2] // 2
    lane = lax.broadcasted_iota(jnp.int32, (tq, 2 * dh), 1)
    zero = jnp.zeros((tq, 2 * dh), BF16)

    lv = lam_ref[...]
    lam = (jnp.exp(jnp.sum(lv[0:1] * lv[1:2], axis=-1, keepdims=True))
           - jnp.exp(jnp.sum(lv[2:3] * lv[3:4], axis=-1, keepdims=True)) + lam_init)

    def attend(exact_max):
        for bb in range(nb):
            q = q_ref[bb]
            qs = jnp.concatenate([jnp.where(lane < dh, q, zero), jnp.where(lane >= dh, q, zero)], axis=0)
            sc = lax.dot_general(qs, k_ref[bb], (((1,), (1,)), ((), ())), preferred_element_type=F32)
            x = sc.reshape(2, tq, s) + bias_ref[...]
            if exact_max:
                x = x - jnp.max(x, axis=-1, keepdims=True)
            e = jnp.exp(x).astype(BF16).reshape(2 * tq, s)
            pv = jnp.dot(e, vx_ref[bb], preferred_element_type=F32)
            r = pv[:, 0:dv] / pv[:, dv:2 * dv]
            o = r[0:tq] - lam * r[tq:2 * tq]
            o_ref[bb] = (_rms_scale(o) * sg_ref[...] * (1.0 - lam_init)).astype(o_ref.dtype)

    @pl.when(flag_ref[0] == 1)
    def _():
        attend(False)

    @pl.when(flag_ref[0] != 1)
    def _():
        attend(True)


def _diff_attn(flag, q, k, vx, bias, lam_vecs, subln_g, *, heads, tq, nb, lam_init):
    bsz, s, qkdim = q.shape
    dh = qkdim // heads // 2
    dv = vx.shape[2] // heads // 2
    kern = functools.partial(_diff_attn_kernel, dh=dh, lam_init=lam_init)
    return pl.pallas_call(
        kern,
        grid=(heads, s // tq, bsz // nb),
        in_specs=[
            pl.BlockSpec(memory_space=pltpu.SMEM),
            pl.BlockSpec((nb, tq, 2 * dh), lambda h, i, b: (b, i, h)),
            pl.BlockSpec((nb, s, 2 * dh), lambda h, i, b: (b, 0, h)),
            pl.BlockSpec((nb, s, 2 * dv), lambda h, i, b: (b, 0, h)),
            pl.BlockSpec((1, tq, s), lambda h, i, b: (h, i, 0)),
            pl.BlockSpec(lam_vecs.shape, lambda h, i, b: (0, 0)),
            pl.BlockSpec((1, dv), lambda h, i, b: (0, 0)),
        ],
        out_specs=pl.BlockSpec((nb, tq, dv), lambda h, i, b: (b, i, h)),
        out_shape=jax.ShapeDtypeStruct((bsz, s, heads * dv), BF16),
        compiler_params=_cparams("parallel", "parallel", "arbitrary"),
        name="diff_attn",
    )(flag, q, k, vx, bias, lam_vecs, subln_g)


def _pick_tile(n, pref):
    t = min(n, pref)
    while n % t:
        t //= 2
    return t


def kernel(x, norm_g, ffn_w_gu, ffn_w_down, gla_w_in, gla_w_gate2, gla_b_gate, gla_o_norm_g, gla_w_out,
           diff_w_in, diff_qk_norm_g, diff_lambda, diff_subln_g, diff_w_out, rel_bias_table):
    bsz, s, d = x.shape
    n = bsz * s
    depth = norm_g.shape[0]
    gla_dv = gla_o_norm_g.shape[1]
    gla_heads = gla_w_out.shape[1] // gla_dv
    dh = diff_qk_norm_g.shape[2]
    diff_heads = rel_bias_table.shape[1]
    diff_qk = diff_heads * 2 * dh
    diff_dv = diff_w_out.shape[1] // diff_heads

    tm = _pick_tile(n, 512)
    tq = _pick_tile(s, 512)
    nb = _pick_tile(bsz, 4)

    w_gu = ffn_w_gu.astype(BF16)
    w_down = ffn_w_down.astype(BF16)
    gla_main = gla_w_in[:, :, :gla_w_in.shape[2] - 2 * gla_w_gate2.shape[2]].astype(BF16)
    gla_lr = gla_w_in[:, :, gla_w_in.shape[2] - 2 * gla_w_gate2.shape[2]:].astype(BF16)
    gla_w2 = gla_w_gate2.astype(BF16)
    gla_wo = gla_w_out.astype(BF16)
    diff_wi = diff_w_in.astype(BF16)
    diff_wo = diff_w_out.astype(BF16)

    grp = jnp.arange(LANES) // dh
    bd = jnp.where(grp[:, None] == grp[None, :], 1.0 / dh, 0.0).astype(BF16)

    table = rel_bias_table.astype(F32)
    g_abs = jnp.max(jnp.abs(diff_qk_norm_g.astype(F32)), axis=-1)
    s_max = jnp.max(g_abs[:, 0] * g_abs[:, 1]) * (dh ** 0.5) * (1.0 + 2.0 ** -6)
    t_max, t_min = jnp.max(table, axis=0), jnp.min(table, axis=0)
    offset_ok = 2.0 * s_max + jnp.max(t_max - t_min) <= DIFF_MAX_GAP
    bias_off = jnp.where(offset_ok, s_max + t_max, 0.0)
    attn_flag = offset_ok.astype(jnp.int32).reshape(1)
    bias = _rel_bias(table, bias_off, s) if depth > 1 else None

    x2 = x.reshape(n, d)
    for i in range(depth):
        j = i // 2
        x2 = _ffn(x2, norm_g[i, 0:1], w_gu, w_down, i, 0, tm=tm)
        g_mix = norm_g[i, 1:2]
        r3 = lambda t: t.reshape(bsz, s, t.shape[1])
        if i % 2 == 0:
            q, k, v, go, laf, lab = _gla_proj(x2, g_mix, gla_main[j], gla_lr[j], gla_w2[j], gla_b_gate[j],
                                              tm=tm, heads=gla_heads)
            y = _gla_core(r3(q), r3(k), r3(v), r3(go), r3(laf), r3(lab), gla_o_norm_g[j:j + 1],
                          heads=gla_heads)
            mix = (y.reshape(n, -1), gla_wo, j)
        else:
            lam_init = 0.8 - 0.6 * math.exp(-0.3 * i)
            gq = jnp.tile(diff_qk_norm_g[j, 0], diff_qk // dh)[None, :]
            gk = jnp.tile(diff_qk_norm_g[j, 1], diff_qk // dh)[None, :]
            q, k, vx = _diff_proj(x2, g_mix, diff_wi[j], gq, gk, bd, tm=tm, dh=dh, dv=diff_dv)
            y = _diff_attn(attn_flag, r3(q), r3(k), r3(vx), bias, diff_lambda[j], diff_subln_g[j:j + 1],
                           heads=diff_heads, tq=tq, nb=nb, lam_init=lam_init)
            mix = (y.reshape(n, -1), diff_wo, j)
        x2 = _ffn(x2, norm_g[i, 2:3], w_gu, w_down, i, 1, tm=tm, mix=mix)
    return x2.reshape(bsz, s, d)
```

```python
import functools
import math

import jax
import jax.numpy as jnp
from jax import lax
from jax.experimental import pallas as pl
from jax.experimental.pallas import tpu as pltpu

F32 = jnp.float32
BF16 = jnp.bfloat16

EPS = 1e-6
GLA_GATE_NORM = 16.0
GLA_CHUNK = 256
GLA_HALF_RANGE = 60.0
DIFF_MAX_GAP = 60.0
REL_BUCKETS = 32
REL_MAX_DIST = 128
LANES = 128
VMEM_LIMIT = 56 * 1024 * 1024
FFN_CHUNK = 256


def _cparams(*sem):
    return pltpu.CompilerParams(dimension_semantics=sem, vmem_limit_bytes=VMEM_LIMIT)


def _rms_scale(x):
    return x * lax.rsqrt(jnp.mean(x * x, axis=-1, keepdims=True) + EPS)


def _resident(block_shape, index_map):
    return pl.BlockSpec(block_shape, index_map, pipeline_mode=pl.Buffered(1))


def _ffn_body(x, g_ref, wg_ref, wu_ref, wd_ref, o_ref, act_ref):
    f = wd_ref.shape[0]
    h = (_rms_scale(x) * g_ref[...]).astype(BF16)
    for c0 in range(0, f, FFN_CHUNK):
        gate = jnp.dot(h, wg_ref[:, c0:c0 + FFN_CHUNK], preferred_element_type=F32)
        up = jnp.dot(h, wu_ref[:, c0:c0 + FFN_CHUNK], preferred_element_type=F32)
        act_ref[:, c0:c0 + FFN_CHUNK] = (gate * jax.nn.sigmoid(gate) * up).astype(BF16)
    o_ref[...] = x + 0.5 * jnp.dot(act_ref[...], wd_ref[...], preferred_element_type=F32)


def _ffn_kernel(x_ref, g_ref, wg_ref, wu_ref, wd_ref, o_ref, act_ref):
    _ffn_body(x_ref[...], g_ref, wg_ref, wu_ref, wd_ref, o_ref, act_ref)


def _mix_ffn_kernel(x_ref, y_ref, wo_ref, g_ref, wg_ref, wu_ref, wd_ref, o_ref, act_ref):
    x = x_ref[...] + jnp.dot(y_ref[...], wo_ref[...], preferred_element_type=F32)
    _ffn_body(x, g_ref, wg_ref, wu_ref, wd_ref, o_ref, act_ref)


def _ffn(x2, g, w_gu, w_down, li, si, *, tm, mix=None):
    n, d = x2.shape
    f = w_down.shape[2]
    row = lambda i: (i, 0)
    ffn_specs = [
        _resident((1, d), lambda i: (0, 0)),
        _resident((None, None, d, f), lambda i: (li, si, 0, 0)),
        _resident((None, None, d, f), lambda i: (li, si, 0, 1)),
        _resident((None, None, f, d), lambda i: (li, si, 0, 0)),
    ]
    if mix is None:
        kern, operands, specs = _ffn_kernel, (x2,), [pl.BlockSpec((tm, d), row)]
    else:
        y2, w_out, lj = mix
        kdim = y2.shape[1]
        kern, operands = _mix_ffn_kernel, (x2, y2, w_out)
        specs = [pl.BlockSpec((tm, d), row), pl.BlockSpec((tm, kdim), row),
                 _resident((None, kdim, d), lambda i: (lj, 0, 0))]
    return pl.pallas_call(
        kern,
        grid=(n // tm,),
        in_specs=specs + ffn_specs,
        out_specs=pl.BlockSpec((tm, d), row),
        out_shape=jax.ShapeDtypeStruct((n, d), F32),
        scratch_shapes=[pltpu.VMEM((tm, f), BF16)],
        compiler_params=_cparams("parallel"),
        name="ffn" if mix is None else "mix_ffn",
    )(*operands, g, w_gu, w_gu, w_down)


def _log_sigmoid(z):
    return jnp.minimum(z, 0.0) - jnp.log(1.0 + jnp.exp(-jnp.abs(z)))


def _gla_proj_kernel(x_ref, g_ref, wm_ref, w2_ref, bg_ref,
                     q_ref, k_ref, v_ref, go_ref, laf_ref, lab_ref, *, qk, vdim, rank, q_scale):
    h = (_rms_scale(x_ref[...]) * g_ref[...]).astype(BF16)
    dot = functools.partial(jnp.dot, preferred_element_type=F32)
    q_ref[...] = dot(h, wm_ref[:, 0:qk]) * q_scale
    k_ref[...] = dot(h, wm_ref[:, qk:2 * qk])
    v_ref[...] = dot(h, wm_ref[:, 2 * qk:2 * qk + vdim]).astype(BF16)
    go_ref[...] = dot(h, wm_ref[:, 2 * qk + vdim:2 * qk + 2 * vdim])
    lr = dot(h, wm_ref[:, 2 * qk + 2 * vdim:2 * qk + 2 * vdim + 2 * rank])
    zf = dot(lr[:, 0:rank].astype(BF16), w2_ref[0]) + bg_ref[0:1, :]
    zb = dot(lr[:, rank:2 * rank].astype(BF16), w2_ref[1]) + bg_ref[1:2, :]
    laf_ref[...] = _log_sigmoid(zf) * (1.0 / GLA_GATE_NORM)
    lab_ref[...] = _log_sigmoid(zb) * (1.0 / GLA_GATE_NORM)


def _gla_proj(x2, g, w_in, lj, w2, bg, *, tm, heads):
    n, d = x2.shape
    rank, qk = w2.shape[1], w2.shape[2]
    vdim = (w_in.shape[2] - 2 * qk - 2 * rank) // 2
    dk = qk // heads
    kern = functools.partial(_gla_proj_kernel, qk=qk, vdim=vdim, rank=rank, q_scale=dk ** -0.5)
    row = lambda i: (i, 0)
    const2 = lambda i: (0, 0)
    return pl.pallas_call(
        kern,
        grid=(n // tm,),
        in_specs=[
            pl.BlockSpec((tm, d), row),
            pl.BlockSpec((1, d), const2),
            _resident((None,) + w_in.shape[1:], lambda i: (lj, 0, 0)),
            pl.BlockSpec(w2.shape, lambda i: (0, 0, 0)),
            pl.BlockSpec(bg.shape, const2),
        ],
        out_specs=[
            pl.BlockSpec((tm, qk), row), pl.BlockSpec((tm, qk), row),
            pl.BlockSpec((tm, vdim), row), pl.BlockSpec((tm, vdim), row),
            pl.BlockSpec((tm, qk), row), pl.BlockSpec((tm, qk), row),
        ],
        out_shape=[
            jax.ShapeDtypeStruct((n, qk), F32), jax.ShapeDtypeStruct((n, qk), F32),
            jax.ShapeDtypeStruct((n, vdim), BF16), jax.ShapeDtypeStruct((n, vdim), F32),
            jax.ShapeDtypeStruct((n, qk), F32), jax.ShapeDtypeStruct((n, qk), F32),
        ],
        compiler_params=_cparams("parallel"),
        name="gla_proj",
    )(x2, g, w_in, w2, bg)


def _cumsum_rows(x, reverse):
    c, w = x.shape
    sub = 8
    t = c // sub
    x3 = x.reshape(t, sub, w)
    row = lax.broadcasted_iota(jnp.int32, (t, sub, w), 1)
    step = 1
    while step < sub:
        if reverse:
            x3 = x3 + jnp.where(row < sub - step, pltpu.roll(x3, sub - step, axis=1), 0.0)
        else:
            x3 = x3 + jnp.where(row >= step, pltpu.roll(x3, step, axis=1), 0.0)
        step *= 2
    order = range(t - 1, -1, -1) if reverse else range(t)
    edge = 0 if reverse else sub - 1
    tiles, carry = [None] * t, None
    for i in order:
        tile = x3[i] if carry is None else x3[i] + carry
        tiles[i] = tile
        carry = tile[edge:edge + 1, :]
    return jnp.concatenate(tiles, axis=0)


def _gla_chunk(q, k, v_bf, la, st, mask, reverse, scr, factorised):
    c = q.shape[0]
    contract_last = (((1,), (1,)), ((), ()))
    b = _cumsum_rows(la, reverse)
    end_row = 0 if reverse else c - 1
    b_end = b[end_row:end_row + 1, :]

    if factorised:
        b_first = b[c - 1 - end_row:c - end_row, :]
        mid = 0.5 * (b_first + b_end)
        qf = q * jnp.exp(b - mid)
        kf = k * jnp.exp(mid - b)
        q_dec = (qf * jnp.exp(mid)).astype(BF16)
        k_dec = (kf * jnp.exp(b_end - mid)).astype(BF16)
        scores = lax.dot_general(qf.astype(BF16), kf.astype(BF16), contract_last,
                                 preferred_element_type=F32)
    else:
        q_dec = (q * jnp.exp(b)).astype(BF16)
        k_dec = (k * jnp.exp(b_end - b)).astype(BF16)
        q_scr, k_scr, b_scr = scr
        q_scr[...] = q
        k_scr[...] = k
        b_scr[...] = b
        lane = lax.broadcasted_iota(jnp.int32, (c, c), 1)

        def col(j, a):
            bj = b_scr[pl.ds(j, 1), :]
            kj = k_scr[pl.ds(j, 1), :]
            t = q_scr[...] * jnp.exp(jnp.minimum(b_scr[...] - bj, 0.0)) * kj
            return jnp.where(lane == j, jnp.sum(t, axis=-1, keepdims=True), a)

        scores = lax.fori_loop(0, c, col, jnp.zeros((c, c), F32))

    scores = jnp.where(mask, scores, 0.0)
    intra = jnp.dot(scores.astype(BF16), v_bf, preferred_element_type=F32)
    inter = lax.dot_general(q_dec, st.astype(BF16), contract_last, preferred_element_type=F32)
    upd = lax.dot_general(v_bf, k_dec, (((0,), (0,)), ((), ())), preferred_element_type=F32)
    return inter + intra, st * jnp.exp(b_end) + upd


def _gla_core_kernel(q_ref, k_ref, v_ref, g_ref, laf_ref, lab_ref, ong_ref, o_ref,
                     of_scr, ob_scr, sf_scr, sb_scr, q_scr, k_scr, b_scr, *, chunk):
    s = q_ref.shape[1]
    n = s // chunk
    row = lax.broadcasted_iota(jnp.int32, (chunk, chunk), 0)
    colm = lax.broadcasted_iota(jnp.int32, (chunk, chunk), 1)
    mask_f = colm <= row
    mask_b = colm > row
    sf_scr[...] = jnp.zeros_like(sf_scr)
    sb_scr[...] = jnp.zeros_like(sb_scr)
    scr = (q_scr, k_scr, b_scr)

    def scan(factorised):
        def body(c, carry):
            rf = pl.ds(pl.multiple_of(c * chunk, chunk), chunk)
            rb = pl.ds(pl.multiple_of((n - 1 - c) * chunk, chunk), chunk)
            args_f = (q_ref[0, rf, :], k_ref[0, rf, :], v_ref[0, rf, :], laf_ref[0, rf, :], sf_scr[...])
            args_b = (q_ref[0, rb, :], k_ref[0, rb, :], v_ref[0, rb, :], lab_ref[0, rb, :], sb_scr[...])
            out_f, st_f = _gla_chunk(*args_f, mask_f, False, scr, factorised)
            out_b, st_b = _gla_chunk(*args_b, mask_b, True, scr, factorised)
            of_scr[rf, :] = out_f
            ob_scr[rb, :] = out_b
            sf_scr[...] = st_f
            sb_scr[...] = st_b
            return carry

        lax.fori_loop(0, n, body, 0, unroll=4 if factorised and n % 4 == 0 else 1)

    la_min = jnp.minimum(jnp.min(laf_ref[0]), jnp.min(lab_ref[0]))
    in_range = la_min * (0.5 * (chunk - 1)) >= -GLA_HALF_RANGE

    @pl.when(in_range)
    def _():
        scan(True)

    @pl.when(jnp.logical_not(in_range))
    def _():
        scan(False)

    def fin(c, carry):
        r = pl.ds(pl.multiple_of(c * chunk, chunk), chunk)
        o = of_scr[r, :] + ob_scr[r, :]
        g = g_ref[0, r, :]
        y = _rms_scale(o) * ong_ref[...] * (g * jax.nn.sigmoid(g))
        o_ref[0, r, :] = y.astype(o_ref.dtype)
        return carry

    lax.fori_loop(0, n, fin, 0, unroll=2 if n % 2 == 0 else 1)


def _gla_core(q, k, v, go, laf, lab, ong, *, heads):
    bsz, s, qk = q.shape
    vdim = v.shape[2]
    dk, dv = qk // heads, vdim // heads
    chunk = min(GLA_CHUNK, s)
    blk = lambda w: pl.BlockSpec((1, s, w), lambda b, h: (b, 0, h))
    kern = functools.partial(_gla_core_kernel, chunk=chunk)
    return pl.pallas_call(
        kern,
        grid=(bsz, heads),
        in_specs=[blk(dk), blk(dk), blk(dv), blk(dv), blk(dk), blk(dk),
                  pl.BlockSpec((1, dv), lambda b, h: (0, 0))],
        out_specs=blk(dv),
        out_shape=jax.ShapeDtypeStruct((bsz, s, vdim), BF16),
        scratch_shapes=[
            pltpu.VMEM((s, dv), F32), pltpu.VMEM((s, dv), F32),
            pltpu.VMEM((dv, dk), F32), pltpu.VMEM((dv, dk), F32),
            pltpu.VMEM((chunk, dk), F32), pltpu.VMEM((chunk, dk), F32),
            pltpu.VMEM((chunk, dk), F32),
        ],
        compiler_params=_cparams("parallel", "parallel"),
        name="gla_core",
    )(q, k, v, go, laf, lab, ong)


def _group_mean_sq(t, bd_ref):
    cols = t.shape[1]
    parts = []
    for c0 in range(0, cols, LANES):
        sq = t[:, c0:c0 + LANES]
        parts.append(jnp.dot((sq * sq).astype(BF16), bd_ref[...], preferred_element_type=F32))
    return jnp.concatenate(parts, axis=1)


def _diff_proj_kernel(x_ref, g_ref, w_ref, gq_ref, gk_ref, bd_ref, q_ref, k_ref, vx_ref, *, qk, dv, q_scale):
    h = (_rms_scale(x_ref[...]) * g_ref[...]).astype(BF16)
    dot = functools.partial(jnp.dot, preferred_element_type=F32)
    q = dot(h, w_ref[:, 0:qk])
    q_ref[...] = (q * lax.rsqrt(_group_mean_sq(q, bd_ref) + EPS) * gq_ref[...] * q_scale).astype(BF16)
    k = dot(h, w_ref[:, qk:2 * qk])
    k_ref[...] = (k * lax.rsqrt(_group_mean_sq(k, bd_ref) + EPS) * gk_ref[...]).astype(BF16)
    v = dot(h, w_ref[:, 2 * qk:]).astype(BF16)
    ones = jnp.ones((v.shape[0], dv), BF16)
    for hh in range(v.shape[1] // dv):
        vx_ref[:, 2 * hh * dv:(2 * hh + 1) * dv] = v[:, hh * dv:(hh + 1) * dv]
        vx_ref[:, (2 * hh + 1) * dv:(2 * hh + 2) * dv] = ones


def _diff_proj(x2, g, w, lj, gq_row, gk_row, bd, *, tm, dh, dv):
    n, d = x2.shape
    qk = gq_row.shape[1]
    vdim = 2 * (w.shape[2] - 2 * qk)
    kern = functools.partial(_diff_proj_kernel, qk=qk, dv=dv, q_scale=dh ** -0.5)
    row = lambda i: (i, 0)
    const2 = lambda i: (0, 0)
    return pl.pallas_call(
        kern,
        grid=(n // tm,),
        in_specs=[
            pl.BlockSpec((tm, d), row),
            pl.BlockSpec((1, d), const2),
            _resident((None,) + w.shape[1:], lambda i: (lj, 0, 0)),
            pl.BlockSpec((1, qk), const2),
            pl.BlockSpec((1, qk), const2),
            pl.BlockSpec(bd.shape, const2),
        ],
        out_specs=[pl.BlockSpec((tm, qk), row), pl.BlockSpec((tm, qk), row), pl.BlockSpec((tm, vdim), row)],
        out_shape=[jax.ShapeDtypeStruct((n, qk), BF16), jax.ShapeDtypeStruct((n, qk), BF16),
                   jax.ShapeDtypeStruct((n, vdim), BF16)],
        compiler_params=_cparams("parallel"),
        name="diff_proj",
    )(x2, g, w, gq_row, gk_row, bd)


def _bucket_thresholds():
    nb = REL_BUCKETS // 2
    max_exact = nb // 2
    m = nb - max_exact
    return [int(math.ceil(max_exact * (REL_MAX_DIST / max_exact) ** (kk / m) - 1e-9)) for kk in range(1, m)]


def _rel_bias_kernel(table_ref, off_ref, o_ref, prof_ref, *, tr):
    heads, _, s = o_ref.shape
    nb = REL_BUCKETS // 2
    max_exact = nb // 2
    i0 = pl.program_id(0) * tr

    @pl.when(pl.program_id(0) == 0)
    def _():
        rel = lax.broadcasted_iota(jnp.int32, (1, 2 * s), 1) - s
        n = jnp.abs(rel)
        large = jnp.full((1, 2 * s), max_exact, jnp.int32)
        for thr in _bucket_thresholds():
            large = large + (n >= thr).astype(jnp.int32)
        bucket = jnp.where(rel > 0, nb, 0) + jnp.where(n < max_exact, n, large)
        for hh in range(heads):
            acc = jnp.zeros((1, 2 * s), F32)
            for u in range(REL_BUCKETS):
                acc = jnp.where(bucket == u, table_ref[u, hh], acc)
            prof_ref[hh:hh + 1, :] = acc - off_ref[hh]

    base = pl.multiple_of(s - tr - i0, tr)
    for hh in range(heads):
        w = prof_ref[hh:hh + 1, pl.ds(base, s + tr)]
        x = pltpu.roll(jnp.broadcast_to(w, (tr, s + tr)), 0, axis=1, stride=1, stride_axis=0)
        o_ref[hh] = x[:, tr:tr + s]


def _rel_bias(table, off, s, *, tr=LANES):
    heads = table.shape[1]
    return pl.pallas_call(
        functools.partial(_rel_bias_kernel, tr=tr),
        grid=(s // tr,),
        in_specs=[pl.BlockSpec(memory_space=pltpu.SMEM), pl.BlockSpec(memory_space=pltpu.SMEM)],
        out_specs=pl.BlockSpec((heads, tr, s), lambda i: (0, i, 0)),
        out_shape=jax.ShapeDtypeStruct((heads, s, s), F32),
        scratch_shapes=[pltpu.VMEM((heads, 2 * s), F32)],
        compiler_params=_cparams("arbitrary"),
        name="rel_bias",
    )(table, off)


def _diff_attn_kernel(flag_ref, q_ref, k_ref, vx_ref, bias_ref, lam_ref, sg_ref, o_ref, *, dh, lam_init):
    nb, tq = q_ref.shape[0], q_ref.shape[1]
    s = k_ref.shape[1]
    dv = vx_ref.shape[2] // 2
    lane = lax.broadcasted_iota(jnp.int32, (tq, 2 * dh), 1)
    zero = jnp.zeros((tq, 2 * dh), BF16)

    lv = lam_ref[...]
    lam = (jnp.exp(jnp.sum(lv[0:1] * lv[1:2], axis=-1, keepdims=True))
           - jnp.exp(jnp.sum(lv[2:3] * lv[3:4], axis=-1, keepdims=True)) + lam_init)

    def attend(exact_max):
        for bb in range(nb):
            q = q_ref[bb]
            qs = jnp.concatenate([jnp.where(lane < dh, q, zero), jnp.where(lane >= dh, q, zero)], axis=0)
            sc = lax.dot_general(qs, k_ref[bb], (((1,), (1,)), ((), ())), preferred_element_type=F32)
            x = sc.reshape(2, tq, s) + bias_ref[...]
            if exact_max:
                x = x - jnp.max(x, axis=-1, keepdims=True)
            e = jnp.exp(x).astype(BF16).reshape(2 * tq, s)
            pv = jnp.dot(e, vx_ref[bb], preferred_element_type=F32)
            r = pv[:, 0:dv] / pv[:, dv:2 * dv]
            o = r[0:tq] - lam * r[tq:2 * tq]
            o_ref[bb] = (_rms_scale(o) * sg_ref[...] * (1.0 - lam_init)).astype(o_ref.dtype)

    @pl.when(flag_ref[0] == 1)
    def _():
        attend(False)

    @pl.when(flag_ref[0] != 1)
    def _():
        attend(True)


def _diff_attn(flag, q, k, vx, bias, lam_vecs, subln_g, *, heads, tq, nb, lam_init):
    bsz, s, qkdim = q.shape
    dh = qkdim // heads // 2
    dv = vx.shape[2] // heads // 2
    kern = functools.partial(_diff_attn_kernel, dh=dh, lam_init=lam_init)
    return pl.pallas_call(
        kern,
        grid=(heads, s // tq, bsz // nb),
        in_specs=[
            pl.BlockSpec(memory_space=pltpu.SMEM),
            pl.BlockSpec((nb, tq, 2 * dh), lambda h, i, b: (b, i, h)),
            pl.BlockSpec((nb, s, 2 * dh), lambda h, i, b: (b, 0, h)),
            pl.BlockSpec((nb, s, 2 * dv), lambda h, i, b: (b, 0, h)),
            pl.BlockSpec((1, tq, s), lambda h, i, b: (h, i, 0)),
            pl.BlockSpec(lam_vecs.shape, lambda h, i, b: (0, 0)),
            pl.BlockSpec((1, dv), lambda h, i, b: (0, 0)),
        ],
        out_specs=pl.BlockSpec((nb, tq, dv), lambda h, i, b: (b, i, h)),
        out_shape=jax.ShapeDtypeStruct((bsz, s, heads * dv), BF16),
        compiler_params=_cparams("parallel", "parallel", "arbitrary"),
        name="diff_attn",
    )(flag, q, k, vx, bias, lam_vecs, subln_g)


def _pick_tile(n, pref):
    t = min(n, pref)
    while n % t:
        t //= 2
    return t


def kernel(x, norm_g, ffn_w_gu, ffn_w_down, gla_w_in, gla_w_gate2, gla_b_gate, gla_o_norm_g, gla_w_out,
           diff_w_in, diff_qk_norm_g, diff_lambda, diff_subln_g, diff_w_out, rel_bias_table):
    bsz, s, d = x.shape
    n = bsz * s
    depth = norm_g.shape[0]
    gla_dv = gla_o_norm_g.shape[1]
    gla_heads = gla_w_out.shape[1] // gla_dv
    dh = diff_qk_norm_g.shape[2]
    diff_heads = rel_bias_table.shape[1]
    diff_qk = diff_heads * 2 * dh
    diff_dv = diff_w_out.shape[1] // diff_heads

    tm = _pick_tile(n, 512)
    tf = _pick_tile(n, 1024)
    tq = _pick_tile(s, 512)
    nb = _pick_tile(bsz, 4)

    w_gu = ffn_w_gu.astype(BF16)
    w_down = ffn_w_down.astype(BF16)
    gla_wi = gla_w_in.astype(BF16)
    gla_w2 = gla_w_gate2.astype(BF16)
    gla_wo = gla_w_out.astype(BF16)
    diff_wi = diff_w_in.astype(BF16)
    diff_wo = diff_w_out.astype(BF16)

    grp = jnp.arange(LANES) // dh
    bd = jnp.where(grp[:, None] == grp[None, :], 1.0 / dh, 0.0).astype(BF16)

    table = rel_bias_table.astype(F32)
    g_abs = jnp.max(jnp.abs(diff_qk_norm_g.astype(F32)), axis=-1)
    s_max = jnp.max(g_abs[:, 0] * g_abs[:, 1]) * (dh ** 0.5) * (1.0 + 2.0 ** -6)
    t_max, t_min = jnp.max(table, axis=0), jnp.min(table, axis=0)
    offset_ok = 2.0 * s_max + jnp.max(t_max - t_min) <= DIFF_MAX_GAP
    bias_off = jnp.where(offset_ok, s_max + t_max, 0.0)
    attn_flag = offset_ok.astype(jnp.int32).reshape(1)
    bias = _rel_bias(table, bias_off, s) if depth > 1 else None

    x2 = x.reshape(n, d)
    for i in range(depth):
        j = i // 2
        x2 = _ffn(x2, norm_g[i, 0:1], w_gu, w_down, i, 0, tm=tf)
        g_mix = norm_g[i, 1:2]
        r3 = lambda t: t.reshape(bsz, s, t.shape[1])
        if i % 2 == 0:
            q, k, v, go, laf, lab = _gla_proj(x2, g_mix, gla_wi, j, gla_w2[j], gla_b_gate[j],
                                              tm=tm, heads=gla_heads)
            y = _gla_core(r3(q), r3(k), r3(v), r3(go), r3(laf), r3(lab), gla_o_norm_g[j:j + 1],
                          heads=gla_heads)
            mix = (y.reshape(n, -1), gla_wo, j)
        else:
            lam_init = 0.8 - 0.6 * math.exp(-0.3 * i)
            gq = jnp.tile(diff_qk_norm_g[j, 0], diff_qk // dh)[None, :]
            gk = jnp.tile(diff_qk_norm_g[j, 1], diff_qk // dh)[None, :]
            q, k, vx = _diff_proj(x2, g_mix, diff_wi, j, gq, gk, bd, tm=tm, dh=dh, dv=diff_dv)
            y = _diff_attn(attn_flag, r3(q), r3(k), r3(vx), bias, diff_lambda[j], diff_subln_g[j:j + 1],
                           heads=diff_heads, tq=tq, nb=nb, lam_init=lam_init)
            mix = (y.reshape(n, -1), diff_wo, j)
        x2 = _ffn(x2, norm_g[i, 2:3], w_gu, w_down, i, 1, tm=tf, mix=mix)
    return x2.reshape(bsz, s, d)
```

```python
import functools
import math

import jax
import jax.numpy as jnp
from jax import lax
from jax.experimental import pallas as pl
from jax.experimental.pallas import tpu as pltpu

F32 = jnp.float32
BF16 = jnp.bfloat16

EPS = 1e-6
GLA_GATE_NORM = 16.0
GLA_CHUNK = 256
GLA_HALF_RANGE = 60.0
DIFF_MAX_GAP = 60.0
REL_BUCKETS = 32
REL_MAX_DIST = 128
LANES = 128
SUBLANES = 8
VMEM_LIMIT = 56 * 1024 * 1024
BF16_NORM_SLACK = 1.0 + 2.0 ** -6
PROJ_ROWS = 512
FFN_ROWS = 1024
ATTN_Q_ROWS = 512
ATTN_BATCH = 4
FFN_CHUNK = 256


def _cparams(*sem):
    return pltpu.CompilerParams(dimension_semantics=sem, vmem_limit_bytes=VMEM_LIMIT)


def _rms_scale(x):
    return x * lax.rsqrt(jnp.mean(x * x, axis=-1, keepdims=True) + EPS)


def _resident(block_shape, index_map):
    return pl.BlockSpec(block_shape, index_map, pipeline_mode=pl.Buffered(1))


def _ffn_body(x, g_ref, wg_ref, wu_ref, wd_ref, o_ref, act_ref):
    f = wd_ref.shape[0]
    h = (_rms_scale(x) * g_ref[...]).astype(BF16)
    for c0 in range(0, f, FFN_CHUNK):
        gate = jnp.dot(h, wg_ref[:, c0:c0 + FFN_CHUNK], preferred_element_type=F32)
        up = jnp.dot(h, wu_ref[:, c0:c0 + FFN_CHUNK], preferred_element_type=F32)
        act_ref[:, c0:c0 + FFN_CHUNK] = (gate * jax.nn.sigmoid(gate) * up).astype(BF16)
    o_ref[...] = x + 0.5 * jnp.dot(act_ref[...], wd_ref[...], preferred_element_type=F32)


def _ffn_kernel(x_ref, g_ref, wg_ref, wu_ref, wd_ref, o_ref, act_ref):
    _ffn_body(x_ref[...], g_ref, wg_ref, wu_ref, wd_ref, o_ref, act_ref)


def _mix_ffn_kernel(x_ref, y_ref, wo_ref, g_ref, wg_ref, wu_ref, wd_ref, o_ref, act_ref):
    x = x_ref[...] + jnp.dot(y_ref[...], wo_ref[...], preferred_element_type=F32)
    _ffn_body(x, g_ref, wg_ref, wu_ref, wd_ref, o_ref, act_ref)


def _ffn(x2, g, w_gu, w_down, li, si, *, tm, mix=None):
    n, d = x2.shape
    f = w_down.shape[2]
    row = lambda i: (i, 0)
    ffn_specs = [
        _resident((1, d), lambda i: (0, 0)),
        _resident((None, None, d, f), lambda i: (li, si, 0, 0)),
        _resident((None, None, d, f), lambda i: (li, si, 0, 1)),
        _resident((None, None, f, d), lambda i: (li, si, 0, 0)),
    ]
    if mix is None:
        kern, operands, specs = _ffn_kernel, (x2,), [pl.BlockSpec((tm, d), row)]
    else:
        y2, w_out, lj = mix
        kdim = y2.shape[1]
        kern, operands = _mix_ffn_kernel, (x2, y2, w_out)
        specs = [pl.BlockSpec((tm, d), row), pl.BlockSpec((tm, kdim), row),
                 _resident((None, kdim, d), lambda i: (lj, 0, 0))]
    return pl.pallas_call(
        kern,
        grid=(n // tm,),
        in_specs=specs + ffn_specs,
        out_specs=pl.BlockSpec((tm, d), row),
        out_shape=jax.ShapeDtypeStruct((n, d), F32),
        scratch_shapes=[pltpu.VMEM((tm, f), BF16)],
        compiler_params=_cparams("parallel"),
        name="ffn" if mix is None else "mix_ffn",
    )(*operands, g, w_gu, w_gu, w_down)


def _log_sigmoid(z):
    return jnp.minimum(z, 0.0) - jnp.log(1.0 + jnp.exp(-jnp.abs(z)))


def _gla_proj_kernel(x_ref, g_ref, wm_ref, w2_ref, bg_ref,
                     q_ref, k_ref, v_ref, go_ref, laf_ref, lab_ref, *, qk, vdim, rank, q_scale):
    h = (_rms_scale(x_ref[...]) * g_ref[...]).astype(BF16)
    dot = functools.partial(jnp.dot, preferred_element_type=F32)
    q_ref[...] = dot(h, wm_ref[:, 0:qk]) * q_scale
    k_ref[...] = dot(h, wm_ref[:, qk:2 * qk])
    v_ref[...] = dot(h, wm_ref[:, 2 * qk:2 * qk + vdim]).astype(BF16)
    go_ref[...] = dot(h, wm_ref[:, 2 * qk + vdim:2 * qk + 2 * vdim])
    lr = dot(h, wm_ref[:, 2 * qk + 2 * vdim:2 * qk + 2 * vdim + 2 * rank])
    zf = dot(lr[:, 0:rank].astype(BF16), w2_ref[0]) + bg_ref[0:1, :]
    zb = dot(lr[:, rank:2 * rank].astype(BF16), w2_ref[1]) + bg_ref[1:2, :]
    laf_ref[...] = _log_sigmoid(zf) * (1.0 / GLA_GATE_NORM)
    lab_ref[...] = _log_sigmoid(zb) * (1.0 / GLA_GATE_NORM)


def _gla_proj(x2, g, w_in, lj, w2, bg, *, tm, heads):
    n, d = x2.shape
    rank, qk = w2.shape[1], w2.shape[2]
    vdim = (w_in.shape[2] - 2 * qk - 2 * rank) // 2
    dk = qk // heads
    kern = functools.partial(_gla_proj_kernel, qk=qk, vdim=vdim, rank=rank, q_scale=dk ** -0.5)
    row = lambda i: (i, 0)
    const2 = lambda i: (0, 0)
    return pl.pallas_call(
        kern,
        grid=(n // tm,),
        in_specs=[
            pl.BlockSpec((tm, d), row),
            pl.BlockSpec((1, d), const2),
            _resident((None,) + w_in.shape[1:], lambda i: (lj, 0, 0)),
            pl.BlockSpec(w2.shape, lambda i: (0, 0, 0)),
            pl.BlockSpec(bg.shape, const2),
        ],
        out_specs=[
            pl.BlockSpec((tm, qk), row), pl.BlockSpec((tm, qk), row),
            pl.BlockSpec((tm, vdim), row), pl.BlockSpec((tm, vdim), row),
            pl.BlockSpec((tm, qk), row), pl.BlockSpec((tm, qk), row),
        ],
        out_shape=[
            jax.ShapeDtypeStruct((n, qk), F32), jax.ShapeDtypeStruct((n, qk), F32),
            jax.ShapeDtypeStruct((n, vdim), BF16), jax.ShapeDtypeStruct((n, vdim), F32),
            jax.ShapeDtypeStruct((n, qk), F32), jax.ShapeDtypeStruct((n, qk), F32),
        ],
        compiler_params=_cparams("parallel"),
        name="gla_proj",
    )(x2, g, w_in, w2, bg)


def _cumsum_rows(x, reverse):
    c, w = x.shape
    sub = SUBLANES
    t = c // sub
    x3 = x.reshape(t, sub, w)
    row = lax.broadcasted_iota(jnp.int32, (t, sub, w), 1)
    step = 1
    while step < sub:
        if reverse:
            x3 = x3 + jnp.where(row < sub - step, pltpu.roll(x3, sub - step, axis=1), 0.0)
        else:
            x3 = x3 + jnp.where(row >= step, pltpu.roll(x3, step, axis=1), 0.0)
        step *= 2
    order = range(t - 1, -1, -1) if reverse else range(t)
    edge = 0 if reverse else sub - 1
    tiles, carry = [None] * t, None
    for i in order:
        tile = x3[i] if carry is None else x3[i] + carry
        tiles[i] = tile
        carry = tile[edge:edge + 1, :]
    return jnp.concatenate(tiles, axis=0)


def _gla_chunk(q, k, v_bf, la, st, mask, reverse, scr, factorised):
    c = q.shape[0]
    contract_last = (((1,), (1,)), ((), ()))
    b = _cumsum_rows(la, reverse)
    end_row = 0 if reverse else c - 1
    b_end = b[end_row:end_row + 1, :]

    if factorised:
        b_first = b[c - 1 - end_row:c - end_row, :]
        mid = 0.5 * (b_first + b_end)
        qf = q * jnp.exp(b - mid)
        kf = k * jnp.exp(mid - b)
        q_dec = (qf * jnp.exp(mid)).astype(BF16)
        k_dec = (kf * jnp.exp(b_end - mid)).astype(BF16)
        scores = lax.dot_general(qf.astype(BF16), kf.astype(BF16), contract_last,
                                 preferred_element_type=F32)
    else:
        q_dec = (q * jnp.exp(b)).astype(BF16)
        k_dec = (k * jnp.exp(b_end - b)).astype(BF16)
        q_scr, k_scr, b_scr = scr
        q_scr[...] = q
        k_scr[...] = k
        b_scr[...] = b
        lane = lax.broadcasted_iota(jnp.int32, (c, c), 1)

        def col(j, a):
            bj = b_scr[pl.ds(j, 1), :]
            kj = k_scr[pl.ds(j, 1), :]
            t = q_scr[...] * jnp.exp(jnp.minimum(b_scr[...] - bj, 0.0)) * kj
            return jnp.where(lane == j, jnp.sum(t, axis=-1, keepdims=True), a)

        scores = lax.fori_loop(0, c, col, jnp.zeros((c, c), F32))

    scores = jnp.where(mask, scores, 0.0)
    intra = jnp.dot(scores.astype(BF16), v_bf, preferred_element_type=F32)
    inter = lax.dot_general(q_dec, st.astype(BF16), contract_last, preferred_element_type=F32)
    upd = lax.dot_general(v_bf, k_dec, (((0,), (0,)), ((), ())), preferred_element_type=F32)
    return inter + intra, st * jnp.exp(b_end) + upd


def _gla_core_kernel(q_ref, k_ref, v_ref, g_ref, laf_ref, lab_ref, ong_ref, o_ref,
                     of_scr, ob_scr, sf_scr, sb_scr, q_scr, k_scr, b_scr, *, chunk):
    s = q_ref.shape[1]
    n = s // chunk
    row = lax.broadcasted_iota(jnp.int32, (chunk, chunk), 0)
    colm = lax.broadcasted_iota(jnp.int32, (chunk, chunk), 1)
    mask_f = colm <= row
    mask_b = colm > row
    sf_scr[...] = jnp.zeros_like(sf_scr)
    sb_scr[...] = jnp.zeros_like(sb_scr)
    scr = (q_scr, k_scr, b_scr)

    def scan(factorised):
        def body(c, carry):
            rf = pl.ds(pl.multiple_of(c * chunk, chunk), chunk)
            rb = pl.ds(pl.multiple_of((n - 1 - c) * chunk, chunk), chunk)
            args_f = (q_ref[0, rf, :], k_ref[0, rf, :], v_ref[0, rf, :], laf_ref[0, rf, :], sf_scr[...])
            args_b = (q_ref[0, rb, :], k_ref[0, rb, :], v_ref[0, rb, :], lab_ref[0, rb, :], sb_scr[...])
            out_f, st_f = _gla_chunk(*args_f, mask_f, False, scr, factorised)
            out_b, st_b = _gla_chunk(*args_b, mask_b, True, scr, factorised)
            of_scr[rf, :] = out_f
            ob_scr[rb, :] = out_b
            sf_scr[...] = st_f
            sb_scr[...] = st_b
            return carry

        lax.fori_loop(0, n, body, 0, unroll=4 if factorised and n % 4 == 0 else 1)

    def chunk_decay_min(la_ref):
        return jnp.min(jnp.sum(la_ref[0].reshape(n, chunk, la_ref.shape[2]), axis=1))

    in_range = jnp.minimum(chunk_decay_min(laf_ref), chunk_decay_min(lab_ref)) >= -2.0 * GLA_HALF_RANGE

    @pl.when(in_range)
    def _():
        scan(True)

    @pl.when(jnp.logical_not(in_range))
    def _():
        scan(False)

    def fin(c, carry):
        r = pl.ds(pl.multiple_of(c * chunk, chunk), chunk)
        o = of_scr[r, :] + ob_scr[r, :]
        g = g_ref[0, r, :]
        y = _rms_scale(o) * ong_ref[...] * (g * jax.nn.sigmoid(g))
        o_ref[0, r, :] = y.astype(o_ref.dtype)
        return carry

    lax.fori_loop(0, n, fin, 0, unroll=2 if n % 2 == 0 else 1)


def _gla_core(q, k, v, go, laf, lab, ong, *, heads):
    bsz, s, qk = q.shape
    vdim = v.shape[2]
    dk, dv = qk // heads, vdim // heads
    chunk = min(GLA_CHUNK, s)
    blk = lambda w: pl.BlockSpec((1, s, w), lambda b, h: (b, 0, h))
    kern = functools.partial(_gla_core_kernel, chunk=chunk)
    return pl.pallas_call(
        kern,
        grid=(bsz, heads),
        in_specs=[blk(dk), blk(dk), blk(dv), blk(dv), blk(dk), blk(dk),
                  pl.BlockSpec((1, dv), lambda b, h: (0, 0))],
        out_specs=blk(dv),
        out_shape=jax.ShapeDtypeStruct((bsz, s, vdim), BF16),
        scratch_shapes=[
            pltpu.VMEM((s, dv), F32), pltpu.VMEM((s, dv), F32),
            pltpu.VMEM((dv, dk), F32), pltpu.VMEM((dv, dk), F32),
            pltpu.VMEM((chunk, dk), F32), pltpu.VMEM((chunk, dk), F32),
            pltpu.VMEM((chunk, dk), F32),
        ],
        compiler_params=_cparams("parallel", "parallel"),
        name="gla_core",
    )(q, k, v, go, laf, lab, ong)


def _group_mean_sq(t, bd_ref):
    cols = t.shape[1]
    parts = []
    for c0 in range(0, cols, LANES):
        sq = t[:, c0:c0 + LANES]
        parts.append(jnp.dot((sq * sq).astype(BF16), bd_ref[...], preferred_element_type=F32))
    return jnp.concatenate(parts, axis=1)


def _diff_proj_kernel(x_ref, g_ref, w_ref, gq_ref, gk_ref, bd_ref, q_ref, k_ref, vx_ref, *, qk, dv, q_scale):
    h = (_rms_scale(x_ref[...]) * g_ref[...]).astype(BF16)
    dot = functools.partial(jnp.dot, preferred_element_type=F32)
    q = dot(h, w_ref[:, 0:qk])
    q_ref[...] = (q * lax.rsqrt(_group_mean_sq(q, bd_ref) + EPS) * gq_ref[...] * q_scale).astype(BF16)
    k = dot(h, w_ref[:, qk:2 * qk])
    k_ref[...] = (k * lax.rsqrt(_group_mean_sq(k, bd_ref) + EPS) * gk_ref[...]).astype(BF16)
    v = dot(h, w_ref[:, 2 * qk:]).astype(BF16)
    ones = jnp.ones((v.shape[0], dv), BF16)
    for hh in range(v.shape[1] // dv):
        vx_ref[:, 2 * hh * dv:(2 * hh + 1) * dv] = v[:, hh * dv:(hh + 1) * dv]
        vx_ref[:, (2 * hh + 1) * dv:(2 * hh + 2) * dv] = ones


def _diff_proj(x2, g, w, lj, gq_row, gk_row, bd, *, tm, dh, dv):
    n, d = x2.shape
    qk = gq_row.shape[1]
    vdim = 2 * (w.shape[2] - 2 * qk)
    kern = functools.partial(_diff_proj_kernel, qk=qk, dv=dv, q_scale=dh ** -0.5)
    row = lambda i: (i, 0)
    const2 = lambda i: (0, 0)
    return pl.pallas_call(
        kern,
        grid=(n // tm,),
        in_specs=[
            pl.BlockSpec((tm, d), row),
            pl.BlockSpec((1, d), const2),
            _resident((None,) + w.shape[1:], lambda i: (lj, 0, 0)),
            pl.BlockSpec((1, qk), const2),
            pl.BlockSpec((1, qk), const2),
            pl.BlockSpec(bd.shape, const2),
        ],
        out_specs=[pl.BlockSpec((tm, qk), row), pl.BlockSpec((tm, qk), row), pl.BlockSpec((tm, vdim), row)],
        out_shape=[jax.ShapeDtypeStruct((n, qk), BF16), jax.ShapeDtypeStruct((n, qk), BF16),
                   jax.ShapeDtypeStruct((n, vdim), BF16)],
        compiler_params=_cparams("parallel"),
        name="diff_proj",
    )(x2, g, w, gq_row, gk_row, bd)


def _bucket_thresholds():
    nb = REL_BUCKETS // 2
    max_exact = nb // 2
    m = nb - max_exact
    return [int(math.ceil(max_exact * (REL_MAX_DIST / max_exact) ** (kk / m) - 1e-9)) for kk in range(1, m)]


def _rel_bias_kernel(table_ref, off_ref, o_ref, prof_ref, *, tr):
    heads, _, s = o_ref.shape
    nb = REL_BUCKETS // 2
    max_exact = nb // 2
    i0 = pl.program_id(0) * tr

    @pl.when(pl.program_id(0) == 0)
    def _():
        rel = lax.broadcasted_iota(jnp.int32, (1, 2 * s), 1) - s
        n = jnp.abs(rel)
        large = jnp.full((1, 2 * s), max_exact, jnp.int32)
        for thr in _bucket_thresholds():
            large = large + (n >= thr).astype(jnp.int32)
        bucket = jnp.where(rel > 0, nb, 0) + jnp.where(n < max_exact, n, large)
        for hh in range(heads):
            acc = jnp.zeros((1, 2 * s), F32)
            for u in range(REL_BUCKETS):
                acc = jnp.where(bucket == u, table_ref[u, hh], acc)
            prof_ref[hh:hh + 1, :] = acc - off_ref[hh]

    base = pl.multiple_of(s - tr - i0, tr)
    for hh in range(heads):
        w = prof_ref[hh:hh + 1, pl.ds(base, s + tr)]
        x = pltpu.roll(jnp.broadcast_to(w, (tr, s + tr)), 0, axis=1, stride=1, stride_axis=0)
        o_ref[hh] = x[:, tr:tr + s]


def _rel_bias(table, off, s, *, tr=LANES):
    heads = table.shape[1]
    return pl.pallas_call(
        functools.partial(_rel_bias_kernel, tr=tr),
        grid=(s // tr,),
        in_specs=[pl.BlockSpec(memory_space=pltpu.SMEM), pl.BlockSpec(memory_space=pltpu.SMEM)],
        out_specs=pl.BlockSpec((heads, tr, s), lambda i: (0, i, 0)),
        out_shape=jax.ShapeDtypeStruct((heads, s, s), F32),
        scratch_shapes=[pltpu.VMEM((heads, 2 * s), F32)],
        compiler_params=_cparams("arbitrary"),
        name="rel_bias",
    )(table, off)


def _diff_attn_kernel(flag_ref, q_ref, k_ref, vx_ref, bias_ref, lam_ref, sg_ref, o_ref, *, dh, lam_init):
    nb, tq = q_ref.shape[0], q_ref.shape[1]
    s = k_ref.shape[1]
    dv = vx_ref.shape[2] // 2
    lane = lax.broadcasted_iota(jnp.int32, (tq, 2 * dh), 1)
    zero = jnp.zeros((tq, 2 * dh), BF16)

    lv = lam_ref[...]
    lam = (jnp.exp(jnp.sum(lv[0:1] * lv[1:2], axis=-1, keepdims=True))
           - jnp.exp(jnp.sum(lv[2:3] * lv[3:4], axis=-1, keepdims=True)) + lam_init)

    def attend(exact_max):
        for bb in range(nb):
            q = q_ref[bb]
            qs = jnp.concatenate([jnp.where(lane < dh, q, zero), jnp.where(lane >= dh, q, zero)], axis=0)
            sc = lax.dot_general(qs, k_ref[bb], (((1,), (1,)), ((), ())), preferred_element_type=F32)
            x = sc.reshape(2, tq, s) + bias_ref[...]
            if exact_max:
                x = x - jnp.max(x, axis=-1, keepdims=True)
            e = jnp.exp(x).astype(BF16).reshape(2 * tq, s)
            pv = jnp.dot(e, vx_ref[bb], preferred_element_type=F32)
            r = pv[:, 0:dv] / pv[:, dv:2 * dv]
            o = r[0:tq] - lam * r[tq:2 * tq]
            o_ref[bb] = (_rms_scale(o) * sg_ref[...] * (1.0 - lam_init)).astype(o_ref.dtype)

    @pl.when(flag_ref[0] == 1)
    def _():
        attend(False)

    @pl.when(flag_ref[0] != 1)
    def _():
        attend(True)


def _diff_attn(flag, q, k, vx, bias, lam_vecs, subln_g, *, heads, tq, nb, lam_init):
    bsz, s, qkdim = q.shape
    dh = qkdim // heads // 2
    dv = vx.shape[2] // heads // 2
    kern = functools.partial(_diff_attn_kernel, dh=dh, lam_init=lam_init)
    return pl.pallas_call(
        kern,
        grid=(heads, s // tq, bsz // nb),
        in_specs=[
            pl.BlockSpec(memory_space=pltpu.SMEM),
            pl.BlockSpec((nb, tq, 2 * dh), lambda h, i, b: (b, i, h)),
            pl.BlockSpec((nb, s, 2 * dh), lambda h, i, b: (b, 0, h)),
            pl.BlockSpec((nb, s, 2 * dv), lambda h, i, b: (b, 0, h)),
            pl.BlockSpec((1, tq, s), lambda h, i, b: (h, i, 0)),
            pl.BlockSpec(lam_vecs.shape, lambda h, i, b: (0, 0)),
            pl.BlockSpec((1, dv), lambda h, i, b: (0, 0)),
        ],
        out_specs=pl.BlockSpec((nb, tq, dv), lambda h, i, b: (b, i, h)),
        out_shape=jax.ShapeDtypeStruct((bsz, s, heads * dv), BF16),
        compiler_params=_cparams("parallel", "parallel", "arbitrary"),
        name="diff_attn",
    )(flag, q, k, vx, bias, lam_vecs, subln_g)


def _pick_tile(n, pref):
    t = min(n, pref)
    while n % t:
        t //= 2
    return t


def kernel(x, norm_g, ffn_w_gu, ffn_w_down, gla_w_in, gla_w_gate2, gla_b_gate, gla_o_norm_g, gla_w_out,
           diff_w_in, diff_qk_norm_g, diff_lambda, diff_subln_g, diff_w_out, rel_bias_table):
    bsz, s, d = x.shape
    n = bsz * s
    depth = norm_g.shape[0]
    gla_dv = gla_o_norm_g.shape[1]
    gla_heads = gla_w_out.shape[1] // gla_dv
    dh = diff_qk_norm_g.shape[2]
    diff_heads = rel_bias_table.shape[1]
    diff_qk = diff_heads * 2 * dh
    diff_dv = diff_w_out.shape[1] // diff_heads

    tm = _pick_tile(n, PROJ_ROWS)
    tf = _pick_tile(n, FFN_ROWS)
    tq = _pick_tile(s, ATTN_Q_ROWS)
    nb = _pick_tile(bsz, ATTN_BATCH)

    w_gu = ffn_w_gu.astype(BF16)
    w_down = ffn_w_down.astype(BF16)
    gla_wi = gla_w_in.astype(BF16)
    gla_w2 = gla_w_gate2.astype(BF16)
    gla_wo = gla_w_out.astype(BF16)
    diff_wi = diff_w_in.astype(BF16)
    diff_wo = diff_w_out.astype(BF16)

    grp = jnp.arange(LANES) // dh
    bd = jnp.where(grp[:, None] == grp[None, :], 1.0 / dh, 0.0).astype(BF16)

    table = rel_bias_table.astype(F32)
    g_abs = jnp.max(jnp.abs(diff_qk_norm_g.astype(F32)), axis=-1)
    s_max = jnp.max(g_abs[:, 0] * g_abs[:, 1]) * (dh ** 0.5) * BF16_NORM_SLACK
    t_max, t_min = jnp.max(table, axis=0), jnp.min(table, axis=0)
    offset_ok = 2.0 * s_max + jnp.max(t_max - t_min) <= DIFF_MAX_GAP
    bias_off = jnp.where(offset_ok, s_max + t_max, 0.0)
    attn_flag = offset_ok.astype(jnp.int32).reshape(1)
    bias = _rel_bias(table, bias_off, s) if depth > 1 else None

    x2 = x.reshape(n, d)
    for i in range(depth):
        j = i // 2
        x2 = _ffn(x2, norm_g[i, 0:1], w_gu, w_down, i, 0, tm=tf)
        g_mix = norm_g[i, 1:2]
        r3 = lambda t: t.reshape(bsz, s, t.shape[1])
        if i % 2 == 0:
            q, k, v, go, laf, lab = _gla_proj(x2, g_mix, gla_wi, j, gla_w2[j], gla_b_gate[j],
                                              tm=tm, heads=gla_heads)
            y = _gla_core(r3(q), r3(k), r3(v), r3(go), r3(laf), r3(lab), gla_o_norm_g[j:j + 1],
                          heads=gla_heads)
            mix = (y.reshape(n, -1), gla_wo, j)
        else:
            lam_init = 0.8 - 0.6 * math.exp(-0.3 * i)
            gq = jnp.tile(diff_qk_norm_g[j, 0], diff_qk // dh)[None, :]
            gk = jnp.tile(diff_qk_norm_g[j, 1], diff_qk // dh)[None, :]
            q, k, vx = _diff_proj(x2, g_mix, diff_wi, j, gq, gk, bd, tm=tm, dh=dh, dv=diff_dv)
            y = _diff_attn(attn_flag, r3(q), r3(k), r3(vx), bias, diff_lambda[j], diff_subln_g[j:j + 1],
                           heads=diff_heads, tq=tq, nb=nb, lam_init=lam_init)
            mix = (y.reshape(n, -1), diff_wo, j)
        x2 = _ffn(x2, norm_g[i, 2:3], w_gu, w_down, i, 1, tm=tf, mix=mix)
    return x2.reshape(bsz, s, d)
```

```python
import functools
import math

import jax
import jax.numpy as jnp
from jax import lax
from jax.experimental import pallas as pl
from jax.experimental.pallas import tpu as pltpu

F32 = jnp.float32
BF16 = jnp.bfloat16

EPS = 1e-6
GLA_GATE_NORM = 16.0
GLA_CHUNK = 256
GLA_HALF_RANGE = 60.0
DIFF_MAX_GAP = 60.0
REL_BUCKETS = 32
REL_MAX_DIST = 128
LANES = 128
SUBLANES = 8
VMEM_LIMIT = 56 * 1024 * 1024
BF16_NORM_SLACK = 1.0 + 2.0 ** -6
PROJ_ROWS = 512
FFN_ROWS = 1024
ATTN_Q_ROWS = 512
ATTN_BATCH = 4
FFN_CHUNK = 256


def _cparams(*sem):
    return pltpu.CompilerParams(dimension_semantics=sem, vmem_limit_bytes=VMEM_LIMIT)


def _rms_scale(x):
    return x * lax.rsqrt(jnp.mean(x * x, axis=-1, keepdims=True) + EPS)


def _resident(block_shape, index_map):
    return pl.BlockSpec(block_shape, index_map, pipeline_mode=pl.Buffered(1))


def _ffn_kernel(*refs, mixed, casting):
    refs = list(refs)
    x = refs.pop(0)[...]
    if mixed:
        y_ref, wo_ref = refs.pop(0), refs.pop(0)
        x = x + jnp.dot(y_ref[...], wo_ref[...], preferred_element_type=F32)
    g_ref, wg_ref, wu_ref, wd_ref = (refs.pop(0) for _ in range(4))
    if casting:
        src_refs = (refs.pop(0), refs.pop(0))
    o_ref = refs.pop(0)
    if casting:
        for src_ref in src_refs:
            refs.pop(0)[...] = src_ref[...].astype(BF16)
    act_ref, = refs

    f = wd_ref.shape[0]
    h = (_rms_scale(x) * g_ref[...]).astype(BF16)
    for c0 in range(0, f, FFN_CHUNK):
        gate = jnp.dot(h, wg_ref[:, c0:c0 + FFN_CHUNK], preferred_element_type=F32)
        up = jnp.dot(h, wu_ref[:, c0:c0 + FFN_CHUNK], preferred_element_type=F32)
        act_ref[:, c0:c0 + FFN_CHUNK] = (gate * jax.nn.sigmoid(gate) * up).astype(BF16)
    o_ref[...] = x + 0.5 * jnp.dot(act_ref[...], wd_ref[...], preferred_element_type=F32)


def _ffn(x2, g, w_gu, w_down, *, tm, mix=None, cast_next=None):
    n, d = x2.shape
    f = w_down.shape[0]
    steps = n // tm
    row = lambda i: (i, 0)
    const = lambda i: (0, 0)
    operands, in_specs = [x2], [pl.BlockSpec((tm, d), row)]
    if mix is not None:
        y2, w_out, lj = mix
        kdim = y2.shape[1]
        operands += [y2, w_out]
        in_specs += [pl.BlockSpec((tm, kdim), row), _resident((None, kdim, d), lambda i: (lj, 0, 0))]
    operands += [g, w_gu, w_gu, w_down]
    in_specs += [_resident((1, d), const),
                 _resident((d, f), const),
                 _resident((d, f), lambda i: (0, 1)),
                 _resident((f, d), const)]
    out_specs = [pl.BlockSpec((tm, d), row)]
    out_shape = [jax.ShapeDtypeStruct((n, d), F32)]
    if cast_next is not None:
        gu32, down32, li, si = cast_next
        rows = d // steps
        assert rows * steps == d and rows % (2 * SUBLANES) == 0, (d, steps)
        for w32 in (gu32, down32):
            cols = w32.shape[3]
            operands.append(w32)
            in_specs.append(pl.BlockSpec((None, None, rows, cols), lambda i: (li, si, i, 0)))
            out_specs.append(pl.BlockSpec((rows, cols), row))
            out_shape.append(jax.ShapeDtypeStruct((d, cols), BF16))
    return pl.pallas_call(
        functools.partial(_ffn_kernel, mixed=mix is not None, casting=cast_next is not None),
        grid=(steps,),
        in_specs=in_specs,
        out_specs=out_specs,
        out_shape=out_shape,
        scratch_shapes=[pltpu.VMEM((tm, f), BF16)],
        compiler_params=_cparams("parallel"),
        name="ffn" if mix is None else "mix_ffn",
    )(*operands)


def _log_sigmoid(z):
    return jnp.minimum(z, 0.0) - jnp.log(1.0 + jnp.exp(-jnp.abs(z)))


def _gla_proj_kernel(x_ref, g_ref, wm_ref, w2_ref, bg_ref,
                     q_ref, k_ref, v_ref, go_ref, laf_ref, lab_ref, *, qk, vdim, rank, q_scale):
    h = (_rms_scale(x_ref[...]) * g_ref[...]).astype(BF16)
    dot = functools.partial(jnp.dot, preferred_element_type=F32)
    q_ref[...] = dot(h, wm_ref[:, 0:qk]) * q_scale
    k_ref[...] = dot(h, wm_ref[:, qk:2 * qk])
    v_ref[...] = dot(h, wm_ref[:, 2 * qk:2 * qk + vdim]).astype(BF16)
    go_ref[...] = dot(h, wm_ref[:, 2 * qk + vdim:2 * qk + 2 * vdim])
    lr = dot(h, wm_ref[:, 2 * qk + 2 * vdim:2 * qk + 2 * vdim + 2 * rank])
    zf = dot(lr[:, 0:rank].astype(BF16), w2_ref[0]) + bg_ref[0:1, :]
    zb = dot(lr[:, rank:2 * rank].astype(BF16), w2_ref[1]) + bg_ref[1:2, :]
    laf_ref[...] = _log_sigmoid(zf) * (1.0 / GLA_GATE_NORM)
    lab_ref[...] = _log_sigmoid(zb) * (1.0 / GLA_GATE_NORM)


def _gla_proj(x2, g, w_in, lj, w2, bg, *, tm, heads):
    n, d = x2.shape
    rank, qk = w2.shape[1], w2.shape[2]
    vdim = (w_in.shape[2] - 2 * qk - 2 * rank) // 2
    dk = qk // heads
    kern = functools.partial(_gla_proj_kernel, qk=qk, vdim=vdim, rank=rank, q_scale=dk ** -0.5)
    row = lambda i: (i, 0)
    const2 = lambda i: (0, 0)
    return pl.pallas_call(
        kern,
        grid=(n // tm,),
        in_specs=[
            pl.BlockSpec((tm, d), row),
            pl.BlockSpec((1, d), const2),
            _resident((None,) + w_in.shape[1:], lambda i: (lj, 0, 0)),
            pl.BlockSpec(w2.shape, lambda i: (0, 0, 0)),
            pl.BlockSpec(bg.shape, const2),
        ],
        out_specs=[
            pl.BlockSpec((tm, qk), row), pl.BlockSpec((tm, qk), row),
            pl.BlockSpec((tm, vdim), row), pl.BlockSpec((tm, vdim), row),
            pl.BlockSpec((tm, qk), row), pl.BlockSpec((tm, qk), row),
        ],
        out_shape=[
            jax.ShapeDtypeStruct((n, qk), F32), jax.ShapeDtypeStruct((n, qk), F32),
            jax.ShapeDtypeStruct((n, vdim), BF16), jax.ShapeDtypeStruct((n, vdim), F32),
            jax.ShapeDtypeStruct((n, qk), F32), jax.ShapeDtypeStruct((n, qk), F32),
        ],
        compiler_params=_cparams("parallel"),
        name="gla_proj",
    )(x2, g, w_in, w2, bg)


def _cumsum_rows(x, reverse):
    c, w = x.shape
    sub = SUBLANES
    t = c // sub
    x3 = x.reshape(t, sub, w)
    row = lax.broadcasted_iota(jnp.int32, (t, sub, w), 1)
    step = 1
    while step < sub:
        if reverse:
            x3 = x3 + jnp.where(row < sub - step, pltpu.roll(x3, sub - step, axis=1), 0.0)
        else:
            x3 = x3 + jnp.where(row >= step, pltpu.roll(x3, step, axis=1), 0.0)
        step *= 2
    order = range(t - 1, -1, -1) if reverse else range(t)
    edge = 0 if reverse else sub - 1
    tiles, carry = [None] * t, None
    for i in order:
        tile = x3[i] if carry is None else x3[i] + carry
        tiles[i] = tile
        carry = tile[edge:edge + 1, :]
    return jnp.concatenate(tiles, axis=0)


def _gla_chunk(q, k, v_bf, la, st, mask, reverse, scr, factorised):
    c = q.shape[0]
    contract_last = (((1,), (1,)), ((), ()))
    b = _cumsum_rows(la, reverse)
    end_row = 0 if reverse else c - 1
    b_end = b[end_row:end_row + 1, :]

    if factorised:
        b_first = b[c - 1 - end_row:c - end_row, :]
        mid = 0.5 * (b_first + b_end)
        qf = q * jnp.exp(b - mid)
        kf = k * jnp.exp(mid - b)
        q_dec = (qf * jnp.exp(mid)).astype(BF16)
        k_dec = (kf * jnp.exp(b_end - mid)).astype(BF16)
        scores = lax.dot_general(qf.astype(BF16), kf.astype(BF16), contract_last,
                                 preferred_element_type=F32)
    else:
        q_dec = (q * jnp.exp(b)).astype(BF16)
        k_dec = (k * jnp.exp(b_end - b)).astype(BF16)
        q_scr, k_scr, b_scr = scr
        q_scr[...] = q
        k_scr[...] = k
        b_scr[...] = b
        lane = lax.broadcasted_iota(jnp.int32, (c, c), 1)

        def col(j, a):
            bj = b_scr[pl.ds(j, 1), :]
            kj = k_scr[pl.ds(j, 1), :]
            t = q_scr[...] * jnp.exp(jnp.minimum(b_scr[...] - bj, 0.0)) * kj
            return jnp.where(lane == j, jnp.sum(t, axis=-1, keepdims=True), a)

        scores = lax.fori_loop(0, c, col, jnp.zeros((c, c), F32))

    scores = jnp.where(mask, scores, 0.0)
    intra = jnp.dot(scores.astype(BF16), v_bf, preferred_element_type=F32)
    inter = lax.dot_general(q_dec, st.astype(BF16), contract_last, preferred_element_type=F32)
    upd = lax.dot_general(v_bf, k_dec, (((0,), (0,)), ((), ())), preferred_element_type=F32)
    return inter + intra, st * jnp.exp(b_end) + upd


def _gla_core_kernel(q_ref, k_ref, v_ref, g_ref, laf_ref, lab_ref, ong_ref, o_ref,
                     of_scr, ob_scr, sf_scr, sb_scr, q_scr, k_scr, b_scr, *, chunk):
    s = q_ref.shape[1]
    n = s // chunk
    row = lax.broadcasted_iota(jnp.int32, (chunk, chunk), 0)
    colm = lax.broadcasted_iota(jnp.int32, (chunk, chunk), 1)
    mask_f = colm <= row
    mask_b = colm > row
    sf_scr[...] = jnp.zeros_like(sf_scr)
    sb_scr[...] = jnp.zeros_like(sb_scr)
    scr = (q_scr, k_scr, b_scr)

    def scan(factorised):
        def body(c, carry):
            rf = pl.ds(pl.multiple_of(c * chunk, chunk), chunk)
            rb = pl.ds(pl.multiple_of((n - 1 - c) * chunk, chunk), chunk)
            args_f = (q_ref[0, rf, :], k_ref[0, rf, :], v_ref[0, rf, :], laf_ref[0, rf, :], sf_scr[...])
            args_b = (q_ref[0, rb, :], k_ref[0, rb, :], v_ref[0, rb, :], lab_ref[0, rb, :], sb_scr[...])
            out_f, st_f = _gla_chunk(*args_f, mask_f, False, scr, factorised)
            out_b, st_b = _gla_chunk(*args_b, mask_b, True, scr, factorised)
            of_scr[rf, :] = out_f
            ob_scr[rb, :] = out_b
            sf_scr[...] = st_f
            sb_scr[...] = st_b
            return carry

        lax.fori_loop(0, n, body, 0, unroll=4 if factorised and n % 4 == 0 else 1)

    def chunk_decay_min(la_ref):
        return jnp.min(jnp.sum(la_ref[0].reshape(n, chunk, la_ref.shape[2]), axis=1))

    in_range = jnp.minimum(chunk_decay_min(laf_ref), chunk_decay_min(lab_ref)) >= -2.0 * GLA_HALF_RANGE

    @pl.when(in_range)
    def _():
        scan(True)

    @pl.when(jnp.logical_not(in_range))
    def _():
        scan(False)

    def fin(c, carry):
        r = pl.ds(pl.multiple_of(c * chunk, chunk), chunk)
        o = of_scr[r, :] + ob_scr[r, :]
        g = g_ref[0, r, :]
        y = _rms_scale(o) * ong_ref[...] * (g * jax.nn.sigmoid(g))
        o_ref[0, r, :] = y.astype(o_ref.dtype)
        return carry

    lax.fori_loop(0, n, fin, 0, unroll=2 if n % 2 == 0 else 1)


def _gla_core(q, k, v, go, laf, lab, ong, *, heads):
    bsz, s, qk = q.shape
    vdim = v.shape[2]
    dk, dv = qk // heads, vdim // heads
    chunk = min(GLA_CHUNK, s)
    blk = lambda w: pl.BlockSpec((1, s, w), lambda b, h: (b, 0, h))
    kern = functools.partial(_gla_core_kernel, chunk=chunk)
    return pl.pallas_call(
        kern,
        grid=(bsz, heads),
        in_specs=[blk(dk), blk(dk), blk(dv), blk(dv), blk(dk), blk(dk),
                  pl.BlockSpec((1, dv), lambda b, h: (0, 0))],
        out_specs=blk(dv),
        out_shape=jax.ShapeDtypeStruct((bsz, s, vdim), BF16),
        scratch_shapes=[
            pltpu.VMEM((s, dv), F32), pltpu.VMEM((s, dv), F32),
            pltpu.VMEM((dv, dk), F32), pltpu.VMEM((dv, dk), F32),
            pltpu.VMEM((chunk, dk), F32), pltpu.VMEM((chunk, dk), F32),
            pltpu.VMEM((chunk, dk), F32),
        ],
        compiler_params=_cparams("parallel", "parallel"),
        name="gla_core",
    )(q, k, v, go, laf, lab, ong)


def _group_mean_sq(t, bd_ref):
    cols = t.shape[1]
    parts = []
    for c0 in range(0, cols, LANES):
        sq = t[:, c0:c0 + LANES]
        parts.append(jnp.dot((sq * sq).astype(BF16), bd_ref[...], preferred_element_type=F32))
    return jnp.concatenate(parts, axis=1)


def _diff_proj_kernel(x_ref, g_ref, w_ref, gq_ref, gk_ref, bd_ref, q_ref, k_ref, vx_ref, *, qk, dv, q_scale):
    h = (_rms_scale(x_ref[...]) * g_ref[...]).astype(BF16)
    dot = functools.partial(jnp.dot, preferred_element_type=F32)
    q = dot(h, w_ref[:, 0:qk])
    q_ref[...] = (q * lax.rsqrt(_group_mean_sq(q, bd_ref) + EPS) * gq_ref[...] * q_scale).astype(BF16)
    k = dot(h, w_ref[:, qk:2 * qk])
    k_ref[...] = (k * lax.rsqrt(_group_mean_sq(k, bd_ref) + EPS) * gk_ref[...]).astype(BF16)
    v = dot(h, w_ref[:, 2 * qk:]).astype(BF16)
    ones = jnp.ones((v.shape[0], dv), BF16)
    for hh in range(v.shape[1] // dv):
        vx_ref[:, 2 * hh * dv:(2 * hh + 1) * dv] = v[:, hh * dv:(hh + 1) * dv]
        vx_ref[:, (2 * hh + 1) * dv:(2 * hh + 2) * dv] = ones


def _diff_proj(x2, g, w, lj, gq_row, gk_row, bd, *, tm, dh, dv):
    n, d = x2.shape
    qk = gq_row.shape[1]
    vdim = 2 * (w.shape[2] - 2 * qk)
    kern = functools.partial(_diff_proj_kernel, qk=qk, dv=dv, q_scale=dh ** -0.5)
    row = lambda i: (i, 0)
    const2 = lambda i: (0, 0)
    return pl.pallas_call(
        kern,
        grid=(n // tm,),
        in_specs=[
            pl.BlockSpec((tm, d), row),
            pl.BlockSpec((1, d), const2),
            _resident((None,) + w.shape[1:], lambda i: (lj, 0, 0)),
            pl.BlockSpec((1, qk), const2),
            pl.BlockSpec((1, qk), const2),
            pl.BlockSpec(bd.shape, const2),
        ],
        out_specs=[pl.BlockSpec((tm, qk), row), pl.BlockSpec((tm, qk), row), pl.BlockSpec((tm, vdim), row)],
        out_shape=[jax.ShapeDtypeStruct((n, qk), BF16), jax.ShapeDtypeStruct((n, qk), BF16),
                   jax.ShapeDtypeStruct((n, vdim), BF16)],
        compiler_params=_cparams("parallel"),
        name="diff_proj",
    )(x2, g, w, gq_row, gk_row, bd)


def _bucket_thresholds():
    nb = REL_BUCKETS // 2
    max_exact = nb // 2
    m = nb - max_exact
    return [int(math.ceil(max_exact * (REL_MAX_DIST / max_exact) ** (kk / m) - 1e-9)) for kk in range(1, m)]


def _rel_bias_kernel(table_ref, off_ref, o_ref, prof_ref, *, tr):
    heads, _, s = o_ref.shape
    nb = REL_BUCKETS // 2
    max_exact = nb // 2
    i0 = pl.program_id(0) * tr

    @pl.when(pl.program_id(0) == 0)
    def _():
        rel = lax.broadcasted_iota(jnp.int32, (1, 2 * s), 1) - s
        n = jnp.abs(rel)
        large = jnp.full((1, 2 * s), max_exact, jnp.int32)
        for thr in _bucket_thresholds():
            large = large + (n >= thr).astype(jnp.int32)
        bucket = jnp.where(rel > 0, nb, 0) + jnp.where(n < max_exact, n, large)
        for hh in range(heads):
            acc = jnp.zeros((1, 2 * s), F32)
            for u in range(REL_BUCKETS):
                acc = jnp.where(bucket == u, table_ref[u, hh], acc)
            prof_ref[hh:hh + 1, :] = acc - off_ref[hh]

    base = pl.multiple_of(s - tr - i0, tr)
    for hh in range(heads):
        w = prof_ref[hh:hh + 1, pl.ds(base, s + tr)]
        x = pltpu.roll(jnp.broadcast_to(w, (tr, s + tr)), 0, axis=1, stride=1, stride_axis=0)
        o_ref[hh] = x[:, tr:tr + s]


def _rel_bias(table, off, s, *, tr=LANES):
    heads = table.shape[1]
    return pl.pallas_call(
        functools.partial(_rel_bias_kernel, tr=tr),
        grid=(s // tr,),
        in_specs=[pl.BlockSpec(memory_space=pltpu.SMEM), pl.BlockSpec(memory_space=pltpu.SMEM)],
        out_specs=pl.BlockSpec((heads, tr, s), lambda i: (0, i, 0)),
        out_shape=jax.ShapeDtypeStruct((heads, s, s), F32),
        scratch_shapes=[pltpu.VMEM((heads, 2 * s), F32)],
        compiler_params=_cparams("arbitrary"),
        name="rel_bias",
    )(table, off)


def _diff_attn_kernel(flag_ref, q_ref, k_ref, vx_ref, bias_ref, lam_ref, sg_ref, o_ref, *, dh, lam_init):
    nb, tq = q_ref.shape[0], q_ref.shape[1]
    s = k_ref.shape[1]
    dv = vx_ref.shape[2] // 2
    lane = lax.broadcasted_iota(jnp.int32, (tq, 2 * dh), 1)
    zero = jnp.zeros((tq, 2 * dh), BF16)

    lv = lam_ref[...]
    lam = (jnp.exp(jnp.sum(lv[0:1] * lv[1:2], axis=-1, keepdims=True))
           - jnp.exp(jnp.sum(lv[2:3] * lv[3:4], axis=-1, keepdims=True)) + lam_init)

    def attend(exact_max):
        for bb in range(nb):
            q = q_ref[bb]
            qs = jnp.concatenate([jnp.where(lane < dh, q, zero), jnp.where(lane >= dh, q, zero)], axis=0)
            sc = lax.dot_general(qs, k_ref[bb], (((1,), (1,)), ((), ())), preferred_element_type=F32)
            x = sc.reshape(2, tq, s) + bias_ref[...]
            if exact_max:
                x = x - jnp.max(x, axis=-1, keepdims=True)
            e = jnp.exp(x).astype(BF16).reshape(2 * tq, s)
            pv = jnp.dot(e, vx_ref[bb], preferred_element_type=F32)
            r = pv[:, 0:dv] / pv[:, dv:2 * dv]
            o = r[0:tq] - lam * r[tq:2 * tq]
            o_ref[bb] = (_rms_scale(o) * sg_ref[...] * (1.0 - lam_init)).astype(o_ref.dtype)

    @pl.when(flag_ref[0] == 1)
    def _():
        attend(False)

    @pl.when(flag_ref[0] != 1)
    def _():
        attend(True)


def _diff_attn(flag, q, k, vx, bias, lam_vecs, subln_g, *, heads, tq, nb, lam_init):
    bsz, s, qkdim = q.shape
    dh = qkdim // heads // 2
    dv = vx.shape[2] // heads // 2
    kern = functools.partial(_diff_attn_kernel, dh=dh, lam_init=lam_init)
    return pl.pallas_call(
        kern,
        grid=(heads, s // tq, bsz // nb),
        in_specs=[
            pl.BlockSpec(memory_space=pltpu.SMEM),
            pl.BlockSpec((nb, tq, 2 * dh), lambda h, i, b: (b, i, h)),
            pl.BlockSpec((nb, s, 2 * dh), lambda h, i, b: (b, 0, h)),
            pl.BlockSpec((nb, s, 2 * dv), lambda h, i, b: (b, 0, h)),
            pl.BlockSpec((1, tq, s), lambda h, i, b: (h, i, 0)),
            pl.BlockSpec(lam_vecs.shape, lambda h, i, b: (0, 0)),
            pl.BlockSpec((1, dv), lambda h, i, b: (0, 0)),
        ],
        out_specs=pl.BlockSpec((nb, tq, dv), lambda h, i, b: (b, i, h)),
        out_shape=jax.ShapeDtypeStruct((bsz, s, heads * dv), BF16),
        compiler_params=_cparams("parallel", "parallel", "arbitrary"),
        name="diff_attn",
    )(flag, q, k, vx, bias, lam_vecs, subln_g)


def _pick_tile(n, pref):
    t = min(n, pref)
    while n % t:
        t //= 2
    return t


def kernel(x, norm_g, ffn_w_gu, ffn_w_down, gla_w_in, gla_w_gate2, gla_b_gate, gla_o_norm_g, gla_w_out,
           diff_w_in, diff_qk_norm_g, diff_lambda, diff_subln_g, diff_w_out, rel_bias_table):
    bsz, s, d = x.shape
    n = bsz * s
    depth = norm_g.shape[0]
    gla_dv = gla_o_norm_g.shape[1]
    gla_heads = gla_w_out.shape[1] // gla_dv
    dh = diff_qk_norm_g.shape[2]
    diff_heads = rel_bias_table.shape[1]
    diff_qk = diff_heads * 2 * dh
    diff_dv = diff_w_out.shape[1] // diff_heads

    tm = _pick_tile(n, PROJ_ROWS)
    tf = _pick_tile(n, FFN_ROWS)
    tq = _pick_tile(s, ATTN_Q_ROWS)
    nb = _pick_tile(bsz, ATTN_BATCH)

    d_ff = ffn_w_down.shape[2]
    down32 = ffn_w_down.reshape(depth, 2, d, d_ff)
    w_gu = ffn_w_gu[0, 0].astype(BF16)
    w_down = ffn_w_down[0, 0].astype(BF16)
    gla_wi = gla_w_in.astype(BF16)
    gla_w2 = gla_w_gate2.astype(BF16)
    gla_wo = gla_w_out.astype(BF16)
    diff_wi = diff_w_in.astype(BF16)
    diff_wo = diff_w_out.astype(BF16)

    grp = jnp.arange(LANES) // dh
    bd = jnp.where(grp[:, None] == grp[None, :], 1.0 / dh, 0.0).astype(BF16)

    table = rel_bias_table.astype(F32)
    g_abs = jnp.max(jnp.abs(diff_qk_norm_g.astype(F32)), axis=-1)
    s_max = jnp.max(g_abs[:, 0] * g_abs[:, 1]) * (dh ** 0.5) * BF16_NORM_SLACK
    t_max, t_min = jnp.max(table, axis=0), jnp.min(table, axis=0)
    offset_ok = 2.0 * s_max + jnp.max(t_max - t_min) <= DIFF_MAX_GAP
    bias_off = jnp.where(offset_ok, s_max + t_max, 0.0)
    attn_flag = offset_ok.astype(jnp.int32).reshape(1)
    bias = _rel_bias(table, bias_off, s) if depth > 1 else None

    x2 = x.reshape(n, d)
    for i in range(depth):
        j = i // 2
        x2, w_gu, w_down = _ffn(x2, norm_g[i, 0:1], w_gu, w_down, tm=tf, cast_next=(ffn_w_gu, down32, i, 1))
        w_down = w_down.reshape(d_ff, d)
        g_mix = norm_g[i, 1:2]
        r3 = lambda t: t.reshape(bsz, s, t.shape[1])
        if i % 2 == 0:
            q, k, v, go, laf, lab = _gla_proj(x2, g_mix, gla_wi, j, gla_w2[j], gla_b_gate[j],
                                              tm=tm, heads=gla_heads)
            y = _gla_core(r3(q), r3(k), r3(v), r3(go), r3(laf), r3(lab), gla_o_norm_g[j:j + 1],
                          heads=gla_heads)
            mix = (y.reshape(n, -1), gla_wo, j)
        else:
            lam_init = 0.8 - 0.6 * math.exp(-0.3 * i)
            gq = jnp.tile(diff_qk_norm_g[j, 0], diff_qk // dh)[None, :]
            gk = jnp.tile(diff_qk_norm_g[j, 1], diff_qk // dh)[None, :]
            q, k, vx = _diff_proj(x2, g_mix, diff_wi, j, gq, gk, bd, tm=tm, dh=dh, dv=diff_dv)
            y = _diff_attn(attn_flag, r3(q), r3(k), r3(vx), bias, diff_lambda[j], diff_subln_g[j:j + 1],
                           heads=diff_heads, tq=tq, nb=nb, lam_init=lam_init)
            mix = (y.reshape(n, -1), diff_wo, j)
        cast_next = (ffn_w_gu, down32, i + 1, 0) if i + 1 < depth else None
        res = _ffn(x2, norm_g[i, 2:3], w_gu, w_down, tm=tf, mix=mix, cast_next=cast_next)
        if cast_next is not None:
            x2, w_gu, w_down = res
            w_down = w_down.reshape(d_ff, d)
        else:
            x2, = res
    return x2.reshape(bsz, s, d)
```

```python
import functools
import math

import jax
import jax.numpy as jnp
from jax import lax
from jax.experimental import pallas as pl
from jax.experimental.pallas import tpu as pltpu

F32 = jnp.float32
BF16 = jnp.bfloat16

EPS = 1e-6
GLA_GATE_NORM = 16.0
GLA_CHUNK = 256
GLA_HALF_RANGE = 60.0
DIFF_MAX_GAP = 60.0
REL_BUCKETS = 32
REL_MAX_DIST = 128
LANES = 128
SUBLANES = 8
VMEM_LIMIT = 56 * 1024 * 1024
BF16_NORM_SLACK = 1.0 + 2.0 ** -6
PROJ_ROWS = 512
FFN_ROWS = 1024
ATTN_Q_ROWS = 512
ATTN_BATCH = 4
FFN_CHUNK = 256


def _cparams(*sem):
    return pltpu.CompilerParams(dimension_semantics=sem, vmem_limit_bytes=VMEM_LIMIT)


def _rms_scale(x):
    return x * lax.rsqrt(jnp.mean(x * x, axis=-1, keepdims=True) + EPS)


def _resident(block_shape, index_map):
    return pl.BlockSpec(block_shape, index_map, pipeline_mode=pl.Buffered(1))


def _ffn_kernel(*refs, mixed, casting):
    refs = list(refs)
    x = refs.pop(0)[...]
    if mixed:
        y_ref, wo_ref = refs.pop(0), refs.pop(0)
        x = x + jnp.dot(y_ref[...], wo_ref[...], preferred_element_type=F32)
    g_ref, wg_ref, wu_ref, wd_ref = (refs.pop(0) for _ in range(4))
    if casting:
        src_refs = (refs.pop(0), refs.pop(0))
    o_ref = refs.pop(0)
    if casting:
        for src_ref in src_refs:
            refs.pop(0)[...] = src_ref[...].astype(BF16)
    act_ref, = refs

    f = wd_ref.shape[0]
    h = (_rms_scale(x) * g_ref[...]).astype(BF16)
    for c0 in range(0, f, FFN_CHUNK):
        gate = jnp.dot(h, wg_ref[:, c0:c0 + FFN_CHUNK], preferred_element_type=F32)
        up = jnp.dot(h, wu_ref[:, c0:c0 + FFN_CHUNK], preferred_element_type=F32)
        act_ref[:, c0:c0 + FFN_CHUNK] = (gate * jax.nn.sigmoid(gate) * up).astype(BF16)
    o_ref[...] = x + 0.5 * jnp.dot(act_ref[...], wd_ref[...], preferred_element_type=F32)


def _ffn(x2, g, w_gu, w_down, *, tm, mix=None, cast_next=None):
    n, d = x2.shape
    f = w_down.shape[0]
    steps = n // tm
    row = lambda i: (i, 0)
    const = lambda i: (0, 0)
    operands, in_specs = [x2], [pl.BlockSpec((tm, d), row)]
    if mix is not None:
        y2, w_out, lj = mix
        kdim = y2.shape[1]
        operands += [y2, w_out]
        in_specs += [pl.BlockSpec((tm, kdim), row), _resident((None, kdim, d), lambda i: (lj, 0, 0))]
    operands += [g, w_gu, w_gu, w_down]
    in_specs += [_resident((1, d), const),
                 _resident((d, f), const),
                 _resident((d, f), lambda i: (0, 1)),
                 _resident((f, d), const)]
    out_specs = [pl.BlockSpec((tm, d), row)]
    out_shape = [jax.ShapeDtypeStruct((n, d), F32)]
    if cast_next is not None:
        gu32, down32, li, si = cast_next
        bf16_rows = 2 * SUBLANES
        for w32 in (gu32, down32):
            nrows, cols = w32.shape[2], w32.shape[3]
            share = next(sh for sh in (1, 2, 4, 8)
                         if steps % sh == 0 and (nrows * sh) % (steps * bf16_rows) == 0)
            rows = nrows * share // steps
            operands.append(w32)
            in_specs.append(pl.BlockSpec((None, None, rows, cols), lambda i, sh=share: (li, si, i // sh, 0)))
            out_specs.append(pl.BlockSpec((rows, cols), lambda i, sh=share: (i // sh, 0)))
            out_shape.append(jax.ShapeDtypeStruct((nrows, cols), BF16))
    return pl.pallas_call(
        functools.partial(_ffn_kernel, mixed=mix is not None, casting=cast_next is not None),
        grid=(steps,),
        in_specs=in_specs,
        out_specs=out_specs,
        out_shape=out_shape,
        scratch_shapes=[pltpu.VMEM((tm, f), BF16)],
        compiler_params=_cparams("arbitrary"),
        name="ffn" if mix is None else "mix_ffn",
    )(*operands)


def _log_sigmoid(z):
    return jnp.minimum(z, 0.0) - jnp.log(1.0 + jnp.exp(-jnp.abs(z)))


def _gla_proj_kernel(x_ref, g_ref, wm_ref, w2_ref, bg_ref,
                     q_ref, k_ref, v_ref, go_ref, laf_ref, lab_ref, *, qk, vdim, rank, q_scale):
    h = (_rms_scale(x_ref[...]) * g_ref[...]).astype(BF16)
    dot = functools.partial(jnp.dot, preferred_element_type=F32)
    q_ref[...] = dot(h, wm_ref[:, 0:qk]) * q_scale
    k_ref[...] = dot(h, wm_ref[:, qk:2 * qk])
    v_ref[...] = dot(h, wm_ref[:, 2 * qk:2 * qk + vdim]).astype(BF16)
    go_ref[...] = dot(h, wm_ref[:, 2 * qk + vdim:2 * qk + 2 * vdim])
    lr = dot(h, wm_ref[:, 2 * qk + 2 * vdim:2 * qk + 2 * vdim + 2 * rank])
    zf = dot(lr[:, 0:rank].astype(BF16), w2_ref[0]) + bg_ref[0:1, :]
    zb = dot(lr[:, rank:2 * rank].astype(BF16), w2_ref[1]) + bg_ref[1:2, :]
    laf_ref[...] = _log_sigmoid(zf) * (1.0 / GLA_GATE_NORM)
    lab_ref[...] = _log_sigmoid(zb) * (1.0 / GLA_GATE_NORM)


def _gla_proj(x2, g, w_in, lj, w2, bg, *, tm, heads):
    n, d = x2.shape
    rank, qk = w2.shape[1], w2.shape[2]
    vdim = (w_in.shape[2] - 2 * qk - 2 * rank) // 2
    dk = qk // heads
    kern = functools.partial(_gla_proj_kernel, qk=qk, vdim=vdim, rank=rank, q_scale=dk ** -0.5)
    row = lambda i: (i, 0)
    const2 = lambda i: (0, 0)
    return pl.pallas_call(
        kern,
        grid=(n // tm,),
        in_specs=[
            pl.BlockSpec((tm, d), row),
            pl.BlockSpec((1, d), const2),
            _resident((None,) + w_in.shape[1:], lambda i: (lj, 0, 0)),
            pl.BlockSpec(w2.shape, lambda i: (0, 0, 0)),
            pl.BlockSpec(bg.shape, const2),
        ],
        out_specs=[
            pl.BlockSpec((tm, qk), row), pl.BlockSpec((tm, qk), row),
            pl.BlockSpec((tm, vdim), row), pl.BlockSpec((tm, vdim), row),
            pl.BlockSpec((tm, qk), row), pl.BlockSpec((tm, qk), row),
        ],
        out_shape=[
            jax.ShapeDtypeStruct((n, qk), F32), jax.ShapeDtypeStruct((n, qk), F32),
            jax.ShapeDtypeStruct((n, vdim), BF16), jax.ShapeDtypeStruct((n, vdim), F32),
            jax.ShapeDtypeStruct((n, qk), F32), jax.ShapeDtypeStruct((n, qk), F32),
        ],
        compiler_params=_cparams("parallel"),
        name="gla_proj",
    )(x2, g, w_in, w2, bg)


def _cumsum_rows(x, reverse):
    c, w = x.shape
    sub = SUBLANES
    t = c // sub
    x3 = x.reshape(t, sub, w)
    row = lax.broadcasted_iota(jnp.int32, (t, sub, w), 1)
    step = 1
    while step < sub:
        if reverse:
            x3 = x3 + jnp.where(row < sub - step, pltpu.roll(x3, sub - step, axis=1), 0.0)
        else:
            x3 = x3 + jnp.where(row >= step, pltpu.roll(x3, step, axis=1), 0.0)
        step *= 2
    order = range(t - 1, -1, -1) if reverse else range(t)
    edge = 0 if reverse else sub - 1
    tiles, carry = [None] * t, None
    for i in order:
        tile = x3[i] if carry is None else x3[i] + carry
        tiles[i] = tile
        carry = tile[edge:edge + 1, :]
    return jnp.concatenate(tiles, axis=0)


def _gla_chunk(q, k, v_bf, la, st, mask, reverse, scr, factorised):
    c = q.shape[0]
    contract_last = (((1,), (1,)), ((), ()))
    b = _cumsum_rows(la, reverse)
    end_row = 0 if reverse else c - 1
    b_end = b[end_row:end_row + 1, :]

    if factorised:
        b_first = b[c - 1 - end_row:c - end_row, :]
        mid = 0.5 * (b_first + b_end)
        qf = q * jnp.exp(b - mid)
        kf = k * jnp.exp(mid - b)
        q_dec = (qf * jnp.exp(mid)).astype(BF16)
        k_dec = (kf * jnp.exp(b_end - mid)).astype(BF16)
        scores = lax.dot_general(qf.astype(BF16), kf.astype(BF16), contract_last,
                                 preferred_element_type=F32)
    else:
        q_dec = (q * jnp.exp(b)).astype(BF16)
        k_dec = (k * jnp.exp(b_end - b)).astype(BF16)
        q_scr, k_scr, b_scr = scr
        q_scr[...] = q
        k_scr[...] = k
        b_scr[...] = b
        lane = lax.broadcasted_iota(jnp.int32, (c, c), 1)

        def col(j, a):
            bj = b_scr[pl.ds(j, 1), :]
            kj = k_scr[pl.ds(j, 1), :]
            t = q_scr[...] * jnp.exp(jnp.minimum(b_scr[...] - bj, 0.0)) * kj
            return jnp.where(lane == j, jnp.sum(t, axis=-1, keepdims=True), a)

        scores = lax.fori_loop(0, c, col, jnp.zeros((c, c), F32))

    scores = jnp.where(mask, scores, 0.0)
    intra = jnp.dot(scores.astype(BF16), v_bf, preferred_element_type=F32)
    inter = lax.dot_general(q_dec, st.astype(BF16), contract_last, preferred_element_type=F32)
    upd = lax.dot_general(v_bf, k_dec, (((0,), (0,)), ((), ())), preferred_element_type=F32)
    return inter + intra, st * jnp.exp(b_end) + upd


def _gla_core_kernel(q_ref, k_ref, v_ref, g_ref, laf_ref, lab_ref, ong_ref, o_ref,
                     of_scr, ob_scr, sf_scr, sb_scr, q_scr, k_scr, b_scr, *, chunk):
    s = q_ref.shape[1]
    n = s // chunk
    row = lax.broadcasted_iota(jnp.int32, (chunk, chunk), 0)
    colm = lax.broadcasted_iota(jnp.int32, (chunk, chunk), 1)
    mask_f = colm <= row
    mask_b = colm > row
    sf_scr[...] = jnp.zeros_like(sf_scr)
    sb_scr[...] = jnp.zeros_like(sb_scr)
    scr = (q_scr, k_scr, b_scr)

    def scan(factorised):
        def body(c, carry):
            rf = pl.ds(pl.multiple_of(c * chunk, chunk), chunk)
            rb = pl.ds(pl.multiple_of((n - 1 - c) * chunk, chunk), chunk)
            args_f = (q_ref[0, rf, :], k_ref[0, rf, :], v_ref[0, rf, :], laf_ref[0, rf, :], sf_scr[...])
            args_b = (q_ref[0, rb, :], k_ref[0, rb, :], v_ref[0, rb, :], lab_ref[0, rb, :], sb_scr[...])
            out_f, st_f = _gla_chunk(*args_f, mask_f, False, scr, factorised)
            out_b, st_b = _gla_chunk(*args_b, mask_b, True, scr, factorised)
            of_scr[rf, :] = out_f
            ob_scr[rb, :] = out_b
            sf_scr[...] = st_f
            sb_scr[...] = st_b
            return carry

        lax.fori_loop(0, n, body, 0, unroll=4 if factorised and n % 4 == 0 else 1)

    def chunk_decay_min(la_ref):
        return jnp.min(jnp.sum(la_ref[0].reshape(n, chunk, la_ref.shape[2]), axis=1))

    in_range = jnp.minimum(chunk_decay_min(laf_ref), chunk_decay_min(lab_ref)) >= -2.0 * GLA_HALF_RANGE

    @pl.when(in_range)
    def _():
        scan(True)

    @pl.when(jnp.logical_not(in_range))
    def _():
        scan(False)

    def fin(c, carry):
        r = pl.ds(pl.multiple_of(c * chunk, chunk), chunk)
        o = of_scr[r, :] + ob_scr[r, :]
        g = g_ref[0, r, :]
        y = _rms_scale(o) * ong_ref[...] * (g * jax.nn.sigmoid(g))
        o_ref[0, r, :] = y.astype(o_ref.dtype)
        return carry

    lax.fori_loop(0, n, fin, 0, unroll=2 if n % 2 == 0 else 1)


def _gla_core(q, k, v, go, laf, lab, ong, *, heads):
    bsz, s, qk = q.shape
    vdim = v.shape[2]
    dk, dv = qk // heads, vdim // heads
    chunk = min(GLA_CHUNK, s)
    blk = lambda w: pl.BlockSpec((1, s, w), lambda b, h: (b, 0, h))
    kern = functools.partial(_gla_core_kernel, chunk=chunk)
    return pl.pallas_call(
        kern,
        grid=(bsz, heads),
        in_specs=[blk(dk), blk(dk), blk(dv), blk(dv), blk(dk), blk(dk),
                  pl.BlockSpec((1, dv), lambda b, h: (0, 0))],
        out_specs=blk(dv),
        out_shape=jax.ShapeDtypeStruct((bsz, s, vdim), BF16),
        scratch_shapes=[
            pltpu.VMEM((s, dv), F32), pltpu.VMEM((s, dv), F32),
            pltpu.VMEM((dv, dk), F32), pltpu.VMEM((dv, dk), F32),
            pltpu.VMEM((chunk, dk), F32), pltpu.VMEM((chunk, dk), F32),
            pltpu.VMEM((chunk, dk), F32),
        ],
        compiler_params=_cparams("parallel", "parallel"),
        name="gla_core",
    )(q, k, v, go, laf, lab, ong)


def _group_mean_sq(t, bd_ref):
    cols = t.shape[1]
    parts = []
    for c0 in range(0, cols, LANES):
        sq = t[:, c0:c0 + LANES]
        parts.append(jnp.dot((sq * sq).astype(BF16), bd_ref[...], preferred_element_type=F32))
    return jnp.concatenate(parts, axis=1)


def _diff_proj_kernel(x_ref, g_ref, w_ref, gq_ref, gk_ref, bd_ref, q_ref, k_ref, vx_ref, *, qk, dv, q_scale):
    h = (_rms_scale(x_ref[...]) * g_ref[...]).astype(BF16)
    dot = functools.partial(jnp.dot, preferred_element_type=F32)
    q = dot(h, w_ref[:, 0:qk])
    q_ref[...] = (q * lax.rsqrt(_group_mean_sq(q, bd_ref) + EPS) * gq_ref[...] * q_scale).astype(BF16)
    k = dot(h, w_ref[:, qk:2 * qk])
    k_ref[...] = (k * lax.rsqrt(_group_mean_sq(k, bd_ref) + EPS) * gk_ref[...]).astype(BF16)
    v = dot(h, w_ref[:, 2 * qk:]).astype(BF16)
    ones = jnp.ones((v.shape[0], dv), BF16)
    for hh in range(v.shape[1] // dv):
        vx_ref[:, 2 * hh * dv:(2 * hh + 1) * dv] = v[:, hh * dv:(hh + 1) * dv]
        vx_ref[:, (2 * hh + 1) * dv:(2 * hh + 2) * dv] = ones


def _diff_proj(x2, g, w, lj, gq_row, gk_row, bd, *, tm, dh, dv):
    n, d = x2.shape
    qk = gq_row.shape[1]
    vdim = 2 * (w.shape[2] - 2 * qk)
    kern = functools.partial(_diff_proj_kernel, qk=qk, dv=dv, q_scale=dh ** -0.5)
    row = lambda i: (i, 0)
    const2 = lambda i: (0, 0)
    return pl.pallas_call(
        kern,
        grid=(n // tm,),
        in_specs=[
            pl.BlockSpec((tm, d), row),
            pl.BlockSpec((1, d), const2),
            _resident((None,) + w.shape[1:], lambda i: (lj, 0, 0)),
            pl.BlockSpec((1, qk), const2),
            pl.BlockSpec((1, qk), const2),
            pl.BlockSpec(bd.shape, const2),
        ],
        out_specs=[pl.BlockSpec((tm, qk), row), pl.BlockSpec((tm, qk), row), pl.BlockSpec((tm, vdim), row)],
        out_shape=[jax.ShapeDtypeStruct((n, qk), BF16), jax.ShapeDtypeStruct((n, qk), BF16),
                   jax.ShapeDtypeStruct((n, vdim), BF16)],
        compiler_params=_cparams("parallel"),
        name="diff_proj",
    )(x2, g, w, gq_row, gk_row, bd)


def _bucket_thresholds():
    nb = REL_BUCKETS // 2
    max_exact = nb // 2
    m = nb - max_exact
    return [int(math.ceil(max_exact * (REL_MAX_DIST / max_exact) ** (kk / m) - 1e-9)) for kk in range(1, m)]


def _rel_bias_kernel(table_ref, off_ref, o_ref, prof_ref, *, tr):
    heads, _, s = o_ref.shape
    nb = REL_BUCKETS // 2
    max_exact = nb // 2
    i0 = pl.program_id(0) * tr

    @pl.when(pl.program_id(0) == 0)
    def _():
        rel = lax.broadcasted_iota(jnp.int32, (1, 2 * s), 1) - s
        n = jnp.abs(rel)
        large = jnp.full((1, 2 * s), max_exact, jnp.int32)
        for thr in _bucket_thresholds():
            large = large + (n >= thr).astype(jnp.int32)
        bucket = jnp.where(rel > 0, nb, 0) + jnp.where(n < max_exact, n, large)
        for hh in range(heads):
            acc = jnp.zeros((1, 2 * s), F32)
            for u in range(REL_BUCKETS):
                acc = jnp.where(bucket == u, table_ref[u, hh], acc)
            prof_ref[hh:hh + 1, :] = acc - off_ref[hh]

    base = pl.multiple_of(s - tr - i0, tr)
    for hh in range(heads):
        w = prof_ref[hh:hh + 1, pl.ds(base, s + tr)]
        x = pltpu.roll(jnp.broadcast_to(w, (tr, s + tr)), 0, axis=1, stride=1, stride_axis=0)
        o_ref[hh] = x[:, tr:tr + s]


def _rel_bias(table, off, s, *, tr=LANES):
    heads = table.shape[1]
    return pl.pallas_call(
        functools.partial(_rel_bias_kernel, tr=tr),
        grid=(s // tr,),
        in_specs=[pl.BlockSpec(memory_space=pltpu.SMEM), pl.BlockSpec(memory_space=pltpu.SMEM)],
        out_specs=pl.BlockSpec((heads, tr, s), lambda i: (0, i, 0)),
        out_shape=jax.ShapeDtypeStruct((heads, s, s), F32),
        scratch_shapes=[pltpu.VMEM((heads, 2 * s), F32)],
        compiler_params=_cparams("arbitrary"),
        name="rel_bias",
    )(table, off)


def _diff_attn_kernel(flag_ref, q_ref, k_ref, vx_ref, bias_ref, lam_ref, sg_ref, o_ref, *, dh, lam_init):
    nb, tq = q_ref.shape[0], q_ref.shape[1]
    s = k_ref.shape[1]
    dv = vx_ref.shape[2] // 2
    lane = lax.broadcasted_iota(jnp.int32, (tq, 2 * dh), 1)
    zero = jnp.zeros((tq, 2 * dh), BF16)

    lv = lam_ref[...]
    lam = (jnp.exp(jnp.sum(lv[0:1] * lv[1:2], axis=-1, keepdims=True))
           - jnp.exp(jnp.sum(lv[2:3] * lv[3:4], axis=-1, keepdims=True)) + lam_init)

    def attend(exact_max):
        for bb in range(nb):
            q = q_ref[bb]
            qs = jnp.concatenate([jnp.where(lane < dh, q, zero), jnp.where(lane >= dh, q, zero)], axis=0)
            sc = lax.dot_general(qs, k_ref[bb], (((1,), (1,)), ((), ())), preferred_element_type=F32)
            x = sc.reshape(2, tq, s) + bias_ref[...]
            if exact_max:
                x = x - jnp.max(x, axis=-1, keepdims=True)
            e = jnp.exp(x).astype(BF16).reshape(2 * tq, s)
            pv = jnp.dot(e, vx_ref[bb], preferred_element_type=F32)
            r = pv[:, 0:dv] / pv[:, dv:2 * dv]
            o = r[0:tq] - lam * r[tq:2 * tq]
            o_ref[bb] = (_rms_scale(o) * sg_ref[...] * (1.0 - lam_init)).astype(o_ref.dtype)

    @pl.when(flag_ref[0] == 1)
    def _():
        attend(False)

    @pl.when(flag_ref[0] != 1)
    def _():
        attend(True)


def _diff_attn(flag, q, k, vx, bias, lam_vecs, subln_g, *, heads, tq, nb, lam_init):
    bsz, s, qkdim = q.shape
    dh = qkdim // heads // 2
    dv = vx.shape[2] // heads // 2
    kern = functools.partial(_diff_attn_kernel, dh=dh, lam_init=lam_init)
    return pl.pallas_call(
        kern,
        grid=(heads, s // tq, bsz // nb),
        in_specs=[
            pl.BlockSpec(memory_space=pltpu.SMEM),
            pl.BlockSpec((nb, tq, 2 * dh), lambda h, i, b: (b, i, h)),
            pl.BlockSpec((nb, s, 2 * dh), lambda h, i, b: (b, 0, h)),
            pl.BlockSpec((nb, s, 2 * dv), lambda h, i, b: (b, 0, h)),
            pl.BlockSpec((1, tq, s), lambda h, i, b: (h, i, 0)),
            pl.BlockSpec(lam_vecs.shape, lambda h, i, b: (0, 0)),
            pl.BlockSpec((1, dv), lambda h, i, b: (0, 0)),
        ],
        out_specs=pl.BlockSpec((nb, tq, dv), lambda h, i, b: (b, i, h)),
        out_shape=jax.ShapeDtypeStruct((bsz, s, heads * dv), BF16),
        compiler_params=_cparams("parallel", "parallel", "arbitrary"),
        name="diff_attn",
    )(flag, q, k, vx, bias, lam_vecs, subln_g)


def _pick_tile(n, pref):
    t = min(n, pref)
    while n % t:
        t //= 2
    return t


def kernel(x, norm_g, ffn_w_gu, ffn_w_down, gla_w_in, gla_w_gate2, gla_b_gate, gla_o_norm_g, gla_w_out,
           diff_w_in, diff_qk_norm_g, diff_lambda, diff_subln_g, diff_w_out, rel_bias_table):
    bsz, s, d = x.shape
    n = bsz * s
    depth = norm_g.shape[0]
    gla_dv = gla_o_norm_g.shape[1]
    gla_heads = gla_w_out.shape[1] // gla_dv
    dh = diff_qk_norm_g.shape[2]
    diff_heads = rel_bias_table.shape[1]
    diff_qk = diff_heads * 2 * dh
    diff_dv = diff_w_out.shape[1] // diff_heads

    tm = _pick_tile(n, PROJ_ROWS)
    tf = _pick_tile(n, FFN_ROWS)
    tq = _pick_tile(s, ATTN_Q_ROWS)
    nb = _pick_tile(bsz, ATTN_BATCH)

    w_gu = ffn_w_gu[0, 0].astype(BF16)
    w_down = ffn_w_down[0, 0].astype(BF16)
    gla_wi = gla_w_in.astype(BF16)
    gla_w2 = gla_w_gate2.astype(BF16)
    gla_wo = gla_w_out.astype(BF16)
    diff_wi = diff_w_in.astype(BF16)
    diff_wo = diff_w_out.astype(BF16)

    grp = jnp.arange(LANES) // dh
    bd = jnp.where(grp[:, None] == grp[None, :], 1.0 / dh, 0.0).astype(BF16)

    table = rel_bias_table.astype(F32)
    g_abs = jnp.max(jnp.abs(diff_qk_norm_g.astype(F32)), axis=-1)
    s_max = jnp.max(g_abs[:, 0] * g_abs[:, 1]) * (dh ** 0.5) * BF16_NORM_SLACK
    t_max, t_min = jnp.max(table, axis=0), jnp.min(table, axis=0)
    offset_ok = 2.0 * s_max + jnp.max(t_max - t_min) <= DIFF_MAX_GAP
    bias_off = jnp.where(offset_ok, s_max + t_max, 0.0)
    attn_flag = offset_ok.astype(jnp.int32).reshape(1)
    bias = _rel_bias(table, bias_off, s) if depth > 1 else None

    x2 = x.reshape(n, d)
    for i in range(depth):
        j = i // 2
        x2, w_gu, w_down = _ffn(x2, norm_g[i, 0:1], w_gu, w_down, tm=tf, cast_next=(ffn_w_gu, ffn_w_down, i, 1))
        g_mix = norm_g[i, 1:2]
        r3 = lambda t: t.reshape(bsz, s, t.shape[1])
        if i % 2 == 0:
            q, k, v, go, laf, lab = _gla_proj(x2, g_mix, gla_wi, j, gla_w2[j], gla_b_gate[j],
                                              tm=tm, heads=gla_heads)
            y = _gla_core(r3(q), r3(k), r3(v), r3(go), r3(laf), r3(lab), gla_o_norm_g[j:j + 1],
                          heads=gla_heads)
            mix = (y.reshape(n, -1), gla_wo, j)
        else:
            lam_init = 0.8 - 0.6 * math.exp(-0.3 * i)
            gq = jnp.tile(diff_qk_norm_g[j, 0], diff_qk // dh)[None, :]
            gk = jnp.tile(diff_qk_norm_g[j, 1], diff_qk // dh)[None, :]
            q, k, vx = _diff_proj(x2, g_mix, diff_wi, j, gq, gk, bd, tm=tm, dh=dh, dv=diff_dv)
            y = _diff_attn(attn_flag, r3(q), r3(k), r3(vx), bias, diff_lambda[j], diff_subln_g[j:j + 1],
                           heads=diff_heads, tq=tq, nb=nb, lam_init=lam_init)
            mix = (y.reshape(n, -1), diff_wo, j)
        cast_next = (ffn_w_gu, ffn_w_down, i + 1, 0) if i + 1 < depth else None
        res = _ffn(x2, norm_g[i, 2:3], w_gu, w_down, tm=tf, mix=mix, cast_next=cast_next)
        if cast_next is not None:
            x2, w_gu, w_down = res
        else:
            x2, = res
    return x2.reshape(bsz, s, d)
```

```python
import functools
import math

import jax
import jax.numpy as jnp
from jax import lax
from jax.experimental import pallas as pl
from jax.experimental.pallas import tpu as pltpu

F32 = jnp.float32
BF16 = jnp.bfloat16

EPS = 1e-6
GLA_GATE_NORM = 16.0
GLA_CHUNK = 256
GLA_HALF_RANGE = 60.0
DIFF_MAX_GAP = 60.0
REL_BUCKETS = 32
REL_MAX_DIST = 128
LANES = 128
SUBLANES = 8
VMEM_LIMIT = 56 * 1024 * 1024
BF16_NORM_SLACK = 1.0 + 2.0 ** -6
PROJ_ROWS = 1024
PROJ_SUB_ROWS = 512
FFN_ROWS = 1024
ATTN_Q_ROWS = 512
ATTN_BATCH = 4
FFN_CHUNK = 256


def _cparams(*sem):
    return pltpu.CompilerParams(dimension_semantics=sem, vmem_limit_bytes=VMEM_LIMIT)


def _rms_scale(x):
    return x * lax.rsqrt(jnp.mean(x * x, axis=-1, keepdims=True) + EPS)


def _resident(block_shape, index_map):
    return pl.BlockSpec(block_shape, index_map, pipeline_mode=pl.Buffered(1))


def _ffn_kernel(*refs, mixed, casting):
    refs = list(refs)
    x = refs.pop(0)[...]
    if mixed:
        y_ref, wo_ref = refs.pop(0), refs.pop(0)
        x = x + jnp.dot(y_ref[...], wo_ref[...], preferred_element_type=F32)
    g_ref, wg_ref, wu_ref, wd_ref = (refs.pop(0) for _ in range(4))
    if casting:
        src_refs = (refs.pop(0), refs.pop(0))
    o_ref = refs.pop(0)
    if casting:
        for src_ref in src_refs:
            refs.pop(0)[...] = src_ref[...].astype(BF16)
    act_ref, = refs

    f = wd_ref.shape[0]
    h = (_rms_scale(x) * g_ref[...]).astype(BF16)
    for c0 in range(0, f, FFN_CHUNK):
        gate = jnp.dot(h, wg_ref[:, c0:c0 + FFN_CHUNK], preferred_element_type=F32)
        up = jnp.dot(h, wu_ref[:, c0:c0 + FFN_CHUNK], preferred_element_type=F32)
        act_ref[:, c0:c0 + FFN_CHUNK] = (gate * jax.nn.sigmoid(gate) * up).astype(BF16)
    o_ref[...] = x + 0.5 * jnp.dot(act_ref[...], wd_ref[...], preferred_element_type=F32)


def _ffn(x2, g, w_gu, w_down, *, tm, mix=None, cast_next=None):
    n, d = x2.shape
    f = w_down.shape[0]
    steps = n // tm
    row = lambda i: (i, 0)
    const = lambda i: (0, 0)
    operands, in_specs = [x2], [pl.BlockSpec((tm, d), row)]
    if mix is not None:
        y2, w_out, lj = mix
        kdim = y2.shape[1]
        operands += [y2, w_out]
        in_specs += [pl.BlockSpec((tm, kdim), row), _resident((None, kdim, d), lambda i: (lj, 0, 0))]
    operands += [g, w_gu, w_gu, w_down]
    in_specs += [_resident((1, d), const),
                 _resident((d, f), const),
                 _resident((d, f), lambda i: (0, 1)),
                 _resident((f, d), const)]
    out_specs = [pl.BlockSpec((tm, d), row)]
    out_shape = [jax.ShapeDtypeStruct((n, d), F32)]
    if cast_next is not None:
        gu32, down32, li, si = cast_next
        bf16_rows = 2 * SUBLANES
        for w32 in (gu32, down32):
            nrows, cols = w32.shape[2], w32.shape[3]
            share = next(sh for sh in (1, 2, 4, 8)
                         if steps % sh == 0 and (nrows * sh) % (steps * bf16_rows) == 0)
            rows = nrows * share // steps
            operands.append(w32)
            in_specs.append(pl.BlockSpec((None, None, rows, cols), lambda i, sh=share: (li, si, i // sh, 0)))
            out_specs.append(pl.BlockSpec((rows, cols), lambda i, sh=share: (i // sh, 0)))
            out_shape.append(jax.ShapeDtypeStruct((nrows, cols), BF16))
    return pl.pallas_call(
        functools.partial(_ffn_kernel, mixed=mix is not None, casting=cast_next is not None),
        grid=(steps,),
        in_specs=in_specs,
        out_specs=out_specs,
        out_shape=out_shape,
        scratch_shapes=[pltpu.VMEM((tm, f), BF16)],
        compiler_params=_cparams("arbitrary"),
        name="ffn" if mix is None else "mix_ffn",
    )(*operands)


def _log_sigmoid(z):
    return jnp.minimum(z, 0.0) - jnp.log(1.0 + jnp.exp(-jnp.abs(z)))


def _gla_proj_kernel(x_ref, g_ref, wm_ref, w2_ref, bg_ref,
                     q_ref, k_ref, v_ref, go_ref, laf_ref, lab_ref, *, qk, vdim, rank, q_scale):
    dot = functools.partial(jnp.dot, preferred_element_type=F32)
    tm = x_ref.shape[0]
    sub = min(PROJ_SUB_ROWS, tm)
    for r0 in range(0, tm, sub):
        rows = slice(r0, r0 + sub)
        h = (_rms_scale(x_ref[rows, :]) * g_ref[...]).astype(BF16)
        q_ref[rows, :] = dot(h, wm_ref[:, 0:qk]) * q_scale
        k_ref[rows, :] = dot(h, wm_ref[:, qk:2 * qk])
        v_ref[rows, :] = dot(h, wm_ref[:, 2 * qk:2 * qk + vdim]).astype(BF16)
        go_ref[rows, :] = dot(h, wm_ref[:, 2 * qk + vdim:2 * qk + 2 * vdim])
        lr = dot(h, wm_ref[:, 2 * qk + 2 * vdim:2 * qk + 2 * vdim + 2 * rank])
        zf = dot(lr[:, 0:rank].astype(BF16), w2_ref[0]) + bg_ref[0:1, :]
        zb = dot(lr[:, rank:2 * rank].astype(BF16), w2_ref[1]) + bg_ref[1:2, :]
        laf_ref[rows, :] = _log_sigmoid(zf) * (1.0 / GLA_GATE_NORM)
        lab_ref[rows, :] = _log_sigmoid(zb) * (1.0 / GLA_GATE_NORM)


def _gla_proj(x2, g, w_in, lj, w2, bg, *, tm, heads):
    n, d = x2.shape
    rank, qk = w2.shape[1], w2.shape[2]
    vdim = (w_in.shape[2] - 2 * qk - 2 * rank) // 2
    dk = qk // heads
    kern = functools.partial(_gla_proj_kernel, qk=qk, vdim=vdim, rank=rank, q_scale=dk ** -0.5)
    row = lambda i: (i, 0)
    const2 = lambda i: (0, 0)
    return pl.pallas_call(
        kern,
        grid=(n // tm,),
        in_specs=[
            pl.BlockSpec((tm, d), row),
            pl.BlockSpec((1, d), const2),
            _resident((None,) + w_in.shape[1:], lambda i: (lj, 0, 0)),
            pl.BlockSpec(w2.shape, lambda i: (0, 0, 0)),
            pl.BlockSpec(bg.shape, const2),
        ],
        out_specs=[
            pl.BlockSpec((tm, qk), row), pl.BlockSpec((tm, qk), row),
            pl.BlockSpec((tm, vdim), row), pl.BlockSpec((tm, vdim), row),
            pl.BlockSpec((tm, qk), row), pl.BlockSpec((tm, qk), row),
        ],
        out_shape=[
            jax.ShapeDtypeStruct((n, qk), F32), jax.ShapeDtypeStruct((n, qk), F32),
            jax.ShapeDtypeStruct((n, vdim), BF16), jax.ShapeDtypeStruct((n, vdim), F32),
            jax.ShapeDtypeStruct((n, qk), F32), jax.ShapeDtypeStruct((n, qk), F32),
        ],
        compiler_params=_cparams("parallel"),
        name="gla_proj",
    )(x2, g, w_in, w2, bg)


def _cumsum_rows(x, reverse):
    c, w = x.shape
    sub = SUBLANES
    t = c // sub
    x3 = x.reshape(t, sub, w)
    row = lax.broadcasted_iota(jnp.int32, (t, sub, w), 1)
    step = 1
    while step < sub:
        if reverse:
            x3 = x3 + jnp.where(row < sub - step, pltpu.roll(x3, sub - step, axis=1), 0.0)
        else:
            x3 = x3 + jnp.where(row >= step, pltpu.roll(x3, step, axis=1), 0.0)
        step *= 2
    order = range(t - 1, -1, -1) if reverse else range(t)
    edge = 0 if reverse else sub - 1
    tiles, carry = [None] * t, None
    for i in order:
        tile = x3[i] if carry is None else x3[i] + carry
        tiles[i] = tile
        carry = tile[edge:edge + 1, :]
    return jnp.concatenate(tiles, axis=0)


def _gla_chunk(q, k, v_bf, la, st, mask, reverse, scr, factorised):
    c = q.shape[0]
    contract_last = (((1,), (1,)), ((), ()))
    b = _cumsum_rows(la, reverse)
    end_row = 0 if reverse else c - 1
    b_end = b[end_row:end_row + 1, :]

    if factorised:
        b_first = b[c - 1 - end_row:c - end_row, :]
        mid = 0.5 * (b_first + b_end)
        qf = q * jnp.exp(b - mid)
        kf = k * jnp.exp(mid - b)
        q_dec = (qf * jnp.exp(mid)).astype(BF16)
        k_dec = (kf * jnp.exp(b_end - mid)).astype(BF16)
        scores = lax.dot_general(qf.astype(BF16), kf.astype(BF16), contract_last,
                                 preferred_element_type=F32)
    else:
        q_dec = (q * jnp.exp(b)).astype(BF16)
        k_dec = (k * jnp.exp(b_end - b)).astype(BF16)
        q_scr, k_scr, b_scr = scr
        q_scr[...] = q
        k_scr[...] = k
        b_scr[...] = b
        lane = lax.broadcasted_iota(jnp.int32, (c, c), 1)

        def col(j, a):
            bj = b_scr[pl.ds(j, 1), :]
            kj = k_scr[pl.ds(j, 1), :]
            t = q_scr[...] * jnp.exp(jnp.minimum(b_scr[...] - bj, 0.0)) * kj
            return jnp.where(lane == j, jnp.sum(t, axis=-1, keepdims=True), a)

        scores = lax.fori_loop(0, c, col, jnp.zeros((c, c), F32))

    scores = jnp.where(mask, scores, 0.0)
    intra = jnp.dot(scores.astype(BF16), v_bf, preferred_element_type=F32)
    inter = lax.dot_general(q_dec, st.astype(BF16), contract_last, preferred_element_type=F32)
    upd = lax.dot_general(v_bf, k_dec, (((0,), (0,)), ((), ())), preferred_element_type=F32)
    return inter + intra, st * jnp.exp(b_end) + upd


def _gla_core_kernel(q_ref, k_ref, v_ref, g_ref, laf_ref, lab_ref, ong_ref, o_ref,
                     of_scr, ob_scr, sf_scr, sb_scr, q_scr, k_scr, b_scr, *, chunk):
    s = q_ref.shape[1]
    n = s // chunk
    row = lax.broadcasted_iota(jnp.int32, (chunk, chunk), 0)
    colm = lax.broadcasted_iota(jnp.int32, (chunk, chunk), 1)
    mask_f = colm <= row
    mask_b = colm > row
    sf_scr[...] = jnp.zeros_like(sf_scr)
    sb_scr[...] = jnp.zeros_like(sb_scr)
    scr = (q_scr, k_scr, b_scr)

    def scan(factorised):
        def body(c, carry):
            rf = pl.ds(pl.multiple_of(c * chunk, chunk), chunk)
            rb = pl.ds(pl.multiple_of((n - 1 - c) * chunk, chunk), chunk)
            args_f = (q_ref[0, rf, :], k_ref[0, rf, :], v_ref[0, rf, :], laf_ref[0, rf, :], sf_scr[...])
            args_b = (q_ref[0, rb, :], k_ref[0, rb, :], v_ref[0, rb, :], lab_ref[0, rb, :], sb_scr[...])
            out_f, st_f = _gla_chunk(*args_f, mask_f, False, scr, factorised)
            out_b, st_b = _gla_chunk(*args_b, mask_b, True, scr, factorised)
            of_scr[rf, :] = out_f
            ob_scr[rb, :] = out_b
            sf_scr[...] = st_f
            sb_scr[...] = st_b
            return carry

        lax.fori_loop(0, n, body, 0, unroll=4 if factorised and n % 4 == 0 else 1)

    def chunk_decay_min(la_ref):
        return jnp.min(jnp.sum(la_ref[0].reshape(n, chunk, la_ref.shape[2]), axis=1))

    in_range = jnp.minimum(chunk_decay_min(laf_ref), chunk_decay_min(lab_ref)) >= -2.0 * GLA_HALF_RANGE

    @pl.when(in_range)
    def _():
        scan(True)

    @pl.when(jnp.logical_not(in_range))
    def _():
        scan(False)

    def fin(c, carry):
        r = pl.ds(pl.multiple_of(c * chunk, chunk), chunk)
        o = of_scr[r, :] + ob_scr[r, :]
        g = g_ref[0, r, :]
        y = _rms_scale(o) * ong_ref[...] * (g * jax.nn.sigmoid(g))
        o_ref[0, r, :] = y.astype(o_ref.dtype)
        return carry

    lax.fori_loop(0, n, fin, 0, unroll=2 if n % 2 == 0 else 1)


def _gla_core(q, k, v, go, laf, lab, ong, *, heads):
    bsz, s, qk = q.shape
    vdim = v.shape[2]
    dk, dv = qk // heads, vdim // heads
    chunk = min(GLA_CHUNK, s)
    blk = lambda w: pl.BlockSpec((1, s, w), lambda b, h: (b, 0, h))
    kern = functools.partial(_gla_core_kernel, chunk=chunk)
    return pl.pallas_call(
        kern,
        grid=(bsz, heads),
        in_specs=[blk(dk), blk(dk), blk(dv), blk(dv), blk(dk), blk(dk),
                  pl.BlockSpec((1, dv), lambda b, h: (0, 0))],
        out_specs=blk(dv),
        out_shape=jax.ShapeDtypeStruct((bsz, s, vdim), BF16),
        scratch_shapes=[
            pltpu.VMEM((s, dv), F32), pltpu.VMEM((s, dv), F32),
            pltpu.VMEM((dv, dk), F32), pltpu.VMEM((dv, dk), F32),
            pltpu.VMEM((chunk, dk), F32), pltpu.VMEM((chunk, dk), F32),
            pltpu.VMEM((chunk, dk), F32),
        ],
        compiler_params=_cparams("parallel", "parallel"),
        name="gla_core",
    )(q, k, v, go, laf, lab, ong)


def _group_mean_sq(t, bd_ref):
    cols = t.shape[1]
    parts = []
    for c0 in range(0, cols, LANES):
        sq = t[:, c0:c0 + LANES]
        parts.append(jnp.dot((sq * sq).astype(BF16), bd_ref[...], preferred_element_type=F32))
    return jnp.concatenate(parts, axis=1)


def _diff_proj_kernel(x_ref, g_ref, w_ref, gq_ref, gk_ref, bd_ref, q_ref, k_ref, vx_ref, *, qk, dv, q_scale):
    dot = functools.partial(jnp.dot, preferred_element_type=F32)
    tm = x_ref.shape[0]
    sub = min(PROJ_SUB_ROWS, tm)
    ones = jnp.ones((sub, dv), BF16)
    for r0 in range(0, tm, sub):
        rows = slice(r0, r0 + sub)
        h = (_rms_scale(x_ref[rows, :]) * g_ref[...]).astype(BF16)
        q = dot(h, w_ref[:, 0:qk])
        q_ref[rows, :] = (q * lax.rsqrt(_group_mean_sq(q, bd_ref) + EPS) * gq_ref[...] * q_scale).astype(BF16)
        k = dot(h, w_ref[:, qk:2 * qk])
        k_ref[rows, :] = (k * lax.rsqrt(_group_mean_sq(k, bd_ref) + EPS) * gk_ref[...]).astype(BF16)
        v = dot(h, w_ref[:, 2 * qk:]).astype(BF16)
        for hh in range(v.shape[1] // dv):
            vx_ref[rows, 2 * hh * dv:(2 * hh + 1) * dv] = v[:, hh * dv:(hh + 1) * dv]
            vx_ref[rows, (2 * hh + 1) * dv:(2 * hh + 2) * dv] = ones


def _diff_proj(x2, g, w, lj, gq_row, gk_row, bd, *, tm, dh, dv):
    n, d = x2.shape
    qk = gq_row.shape[1]
    vdim = 2 * (w.shape[2] - 2 * qk)
    kern = functools.partial(_diff_proj_kernel, qk=qk, dv=dv, q_scale=dh ** -0.5)
    row = lambda i: (i, 0)
    const2 = lambda i: (0, 0)
    return pl.pallas_call(
        kern,
        grid=(n // tm,),
        in_specs=[
            pl.BlockSpec((tm, d), row),
            pl.BlockSpec((1, d), const2),
            _resident((None,) + w.shape[1:], lambda i: (lj, 0, 0)),
            pl.BlockSpec((1, qk), const2),
            pl.BlockSpec((1, qk), const2),
            pl.BlockSpec(bd.shape, const2),
        ],
        out_specs=[pl.BlockSpec((tm, qk), row), pl.BlockSpec((tm, qk), row), pl.BlockSpec((tm, vdim), row)],
        out_shape=[jax.ShapeDtypeStruct((n, qk), BF16), jax.ShapeDtypeStruct((n, qk), BF16),
                   jax.ShapeDtypeStruct((n, vdim), BF16)],
        compiler_params=_cparams("parallel"),
        name="diff_proj",
    )(x2, g, w, gq_row, gk_row, bd)


def _bucket_thresholds():
    nb = REL_BUCKETS // 2
    max_exact = nb // 2
    m = nb - max_exact
    return [int(math.ceil(max_exact * (REL_MAX_DIST / max_exact) ** (kk / m) - 1e-9)) for kk in range(1, m)]


def _rel_bias_kernel(table_ref, off_ref, o_ref, prof_ref, *, tr):
    heads, _, s = o_ref.shape
    nb = REL_BUCKETS // 2
    max_exact = nb // 2
    i0 = pl.program_id(0) * tr

    @pl.when(pl.program_id(0) == 0)
    def _():
        rel = lax.broadcasted_iota(jnp.int32, (1, 2 * s), 1) - s
        n = jnp.abs(rel)
        large = jnp.full((1, 2 * s), max_exact, jnp.int32)
        for thr in _bucket_thresholds():
            large = large + (n >= thr).astype(jnp.int32)
        bucket = jnp.where(rel > 0, nb, 0) + jnp.where(n < max_exact, n, large)
        for hh in range(heads):
            acc = jnp.zeros((1, 2 * s), F32)
            for u in range(REL_BUCKETS):
                acc = jnp.where(bucket == u, table_ref[u, hh], acc)
            prof_ref[hh:hh + 1, :] = acc - off_ref[hh]

    base = pl.multiple_of(s - tr - i0, tr)
    for hh in range(heads):
        w = prof_ref[hh:hh + 1, pl.ds(base, s + tr)]
        x = pltpu.roll(jnp.broadcast_to(w, (tr, s + tr)), 0, axis=1, stride=1, stride_axis=0)
        o_ref[hh] = x[:, tr:tr + s]


def _rel_bias(table, off, s, *, tr=LANES):
    heads = table.shape[1]
    return pl.pallas_call(
        functools.partial(_rel_bias_kernel, tr=tr),
        grid=(s // tr,),
        in_specs=[pl.BlockSpec(memory_space=pltpu.SMEM), pl.BlockSpec(memory_space=pltpu.SMEM)],
        out_specs=pl.BlockSpec((heads, tr, s), lambda i: (0, i, 0)),
        out_shape=jax.ShapeDtypeStruct((heads, s, s), F32),
        scratch_shapes=[pltpu.VMEM((heads, 2 * s), F32)],
        compiler_params=_cparams("arbitrary"),
        name="rel_bias",
    )(table, off)


def _diff_attn_kernel(flag_ref, q_ref, k_ref, vx_ref, bias_ref, lam_ref, sg_ref, o_ref, *, dh, lam_init):
    nb, tq = q_ref.shape[0], q_ref.shape[1]
    s = k_ref.shape[1]
    dv = vx_ref.shape[2] // 2
    lane = lax.broadcasted_iota(jnp.int32, (tq, 2 * dh), 1)
    zero = jnp.zeros((tq, 2 * dh), BF16)

    lv = lam_ref[...]
    lam = (jnp.exp(jnp.sum(lv[0:1] * lv[1:2], axis=-1, keepdims=True))
           - jnp.exp(jnp.sum(lv[2:3] * lv[3:4], axis=-1, keepdims=True)) + lam_init)

    def attend(exact_max):
        for bb in range(nb):
            q = q_ref[bb]
            qs = jnp.concatenate([jnp.where(lane < dh, q, zero), jnp.where(lane >= dh, q, zero)], axis=0)
            sc = lax.dot_general(qs, k_ref[bb], (((1,), (1,)), ((), ())), preferred_element_type=F32)
            x = sc.reshape(2, tq, s) + bias_ref[...]
            if exact_max:
                x = x - jnp.max(x, axis=-1, keepdims=True)
            e = jnp.exp(x).astype(BF16).reshape(2 * tq, s)
            pv = jnp.dot(e, vx_ref[bb], preferred_element_type=F32)
            r = pv[:, 0:dv] / pv[:, dv:2 * dv]
            o = r[0:tq] - lam * r[tq:2 * tq]
            o_ref[bb] = (_rms_scale(o) * sg_ref[...] * (1.0 - lam_init)).astype(o_ref.dtype)

    @pl.when(flag_ref[0] == 1)
    def _():
        attend(False)

    @pl.when(flag_ref[0] != 1)
    def _():
        attend(True)


def _diff_attn(flag, q, k, vx, bias, lam_vecs, subln_g, *, heads, tq, nb, lam_init):
    bsz, s, qkdim = q.shape
    dh = qkdim // heads // 2
    dv = vx.shape[2] // heads // 2
    kern = functools.partial(_diff_attn_kernel, dh=dh, lam_init=lam_init)
    return pl.pallas_call(
        kern,
        grid=(heads, s // tq, bsz // nb),
        in_specs=[
            pl.BlockSpec(memory_space=pltpu.SMEM),
            pl.BlockSpec((nb, tq, 2 * dh), lambda h, i, b: (b, i, h)),
            pl.BlockSpec((nb, s, 2 * dh), lambda h, i, b: (b, 0, h)),
            pl.BlockSpec((nb, s, 2 * dv), lambda h, i, b: (b, 0, h)),
            pl.BlockSpec((1, tq, s), lambda h, i, b: (h, i, 0)),
            pl.BlockSpec(lam_vecs.shape, lambda h, i, b: (0, 0)),
            pl.BlockSpec((1, dv), lambda h, i, b: (0, 0)),
        ],
        out_specs=pl.BlockSpec((nb, tq, dv), lambda h, i, b: (b, i, h)),
        out_shape=jax.ShapeDtypeStruct((bsz, s, heads * dv), BF16),
        compiler_params=_cparams("parallel", "parallel", "arbitrary"),
        name="diff_attn",
    )(flag, q, k, vx, bias, lam_vecs, subln_g)


def _pick_tile(n, pref):
    t = min(n, pref)
    while n % t:
        t //= 2
    return t


def kernel(x, norm_g, ffn_w_gu, ffn_w_down, gla_w_in, gla_w_gate2, gla_b_gate, gla_o_norm_g, gla_w_out,
           diff_w_in, diff_qk_norm_g, diff_lambda, diff_subln_g, diff_w_out, rel_bias_table):
    bsz, s, d = x.shape
    n = bsz * s
    depth = norm_g.shape[0]
    gla_dv = gla_o_norm_g.shape[1]
    gla_heads = gla_w_out.shape[1] // gla_dv
    dh = diff_qk_norm_g.shape[2]
    diff_heads = rel_bias_table.shape[1]
    diff_qk = diff_heads * 2 * dh
    diff_dv = diff_w_out.shape[1] // diff_heads

    tm = _pick_tile(n, PROJ_ROWS)
    tf = _pick_tile(n, FFN_ROWS)
    tq = _pick_tile(s, ATTN_Q_ROWS)
    nb = _pick_tile(bsz, ATTN_BATCH)

    w_gu = ffn_w_gu[0, 0].astype(BF16)
    w_down = ffn_w_down[0, 0].astype(BF16)
    gla_wi = gla_w_in.astype(BF16)
    gla_w2 = gla_w_gate2.astype(BF16)
    gla_wo = gla_w_out.astype(BF16)
    diff_wi = diff_w_in.astype(BF16)
    diff_wo = diff_w_out.astype(BF16)

    grp = jnp.arange(LANES) // dh
    bd = jnp.where(grp[:, None] == grp[None, :], 1.0 / dh, 0.0).astype(BF16)

    table = rel_bias_table.astype(F32)
    g_abs = jnp.max(jnp.abs(diff_qk_norm_g.astype(F32)), axis=-1)
    s_max = jnp.max(g_abs[:, 0] * g_abs[:, 1]) * (dh ** 0.5) * BF16_NORM_SLACK
    t_max, t_min = jnp.max(table, axis=0), jnp.min(table, axis=0)
    offset_ok = 2.0 * s_max + jnp.max(t_max - t_min) <= DIFF_MAX_GAP
    bias_off = jnp.where(offset_ok, s_max + t_max, 0.0)
    attn_flag = offset_ok.astype(jnp.int32).reshape(1)
    bias = _rel_bias(table, bias_off, s) if depth > 1 else None

    x2 = x.reshape(n, d)
    for i in range(depth):
        j = i // 2
        x2, w_gu, w_down = _ffn(x2, norm_g[i, 0:1], w_gu, w_down, tm=tf, cast_next=(ffn_w_gu, ffn_w_down, i, 1))
        g_mix = norm_g[i, 1:2]
        r3 = lambda t: t.reshape(bsz, s, t.shape[1])
        if i % 2 == 0:
            q, k, v, go, laf, lab = _gla_proj(x2, g_mix, gla_wi, j, gla_w2[j], gla_b_gate[j],
                                              tm=tm, heads=gla_heads)
            y = _gla_core(r3(q), r3(k), r3(v), r3(go), r3(laf), r3(lab), gla_o_norm_g[j:j + 1],
                          heads=gla_heads)
            mix = (y.reshape(n, -1), gla_wo, j)
        else:
            lam_init = 0.8 - 0.6 * math.exp(-0.3 * i)
            gq = jnp.tile(diff_qk_norm_g[j, 0], diff_qk // dh)[None, :]
            gk = jnp.tile(diff_qk_norm_g[j, 1], diff_qk // dh)[None, :]
            q, k, vx = _diff_proj(x2, g_mix, diff_wi, j, gq, gk, bd, tm=tm, dh=dh, dv=diff_dv)
            y = _diff_attn(attn_flag, r3(q), r3(k), r3(vx), bias, diff_lambda[j], diff_subln_g[j:j + 1],
                           heads=diff_heads, tq=tq, nb=nb, lam_init=lam_init)
            mix = (y.reshape(n, -1), diff_wo, j)
        cast_next = (ffn_w_gu, ffn_w_down, i + 1, 0) if i + 1 < depth else None
        res = _ffn(x2, norm_g[i, 2:3], w_gu, w_down, tm=tf, mix=mix, cast_next=cast_next)
        if cast_next is not None:
            x2, w_gu, w_down = res
        else:
            x2, = res
    return x2.reshape(bsz, s, d)
```

```python
import functools
import math

import jax
import jax.numpy as jnp
from jax import lax
from jax.experimental import pallas as pl
from jax.experimental.pallas import tpu as pltpu

F32 = jnp.float32
BF16 = jnp.bfloat16

EPS = 1e-6
GLA_GATE_NORM = 16.0
GLA_CHUNK = 256
GLA_HALF_RANGE = 60.0
DIFF_MAX_GAP = 60.0
REL_BUCKETS = 32
REL_MAX_DIST = 128
LANES = 128
SUBLANES = 8
VMEM_LIMIT = 56 * 1024 * 1024
BF16_NORM_SLACK = 1.0 + 2.0 ** -6
PROJ_ROWS = 1024
PROJ_SUB_ROWS = 512
FFN_ROWS = 1024
ATTN_Q_ROWS = 512
ATTN_BATCH = 4
FFN_CHUNK = 256


def _cparams(*sem):
    return pltpu.CompilerParams(dimension_semantics=sem, vmem_limit_bytes=VMEM_LIMIT)


def _rms_scale(x):
    return x * lax.rsqrt(jnp.mean(x * x, axis=-1, keepdims=True) + EPS)


def _resident(block_shape, index_map):
    return pl.BlockSpec(block_shape, index_map, pipeline_mode=pl.Buffered(1))


def _ffn_kernel(*refs, mixed, casting):
    refs = list(refs)
    x = refs.pop(0)[...]
    if mixed:
        y_ref, wo_ref = refs.pop(0), refs.pop(0)
        x = x + jnp.dot(y_ref[...], wo_ref[...], preferred_element_type=F32)
    g_ref, wg_ref, wu_ref, wd_ref = (refs.pop(0) for _ in range(4))
    src_refs = [refs.pop(0) for _ in range(casting)]
    o_ref = refs.pop(0)
    for src_ref in src_refs:
        refs.pop(0)[...] = src_ref[...].astype(BF16)
    act_ref, = refs

    f = wd_ref.shape[0]
    h = (_rms_scale(x) * g_ref[...]).astype(BF16)
    for c0 in range(0, f, FFN_CHUNK):
        gate = jnp.dot(h, wg_ref[:, c0:c0 + FFN_CHUNK], preferred_element_type=F32)
        up = jnp.dot(h, wu_ref[:, c0:c0 + FFN_CHUNK], preferred_element_type=F32)
        act_ref[:, c0:c0 + FFN_CHUNK] = (gate * jax.nn.sigmoid(gate) * up).astype(BF16)
    o_ref[...] = x + 0.5 * jnp.dot(act_ref[...], wd_ref[...], preferred_element_type=F32)


def _ffn(x2, g, w_gu, w_down, *, tm, mix=None, cast_next=None, cast_mixer=()):
    n, d = x2.shape
    f = w_down.shape[0]
    steps = n // tm
    row = lambda i: (i, 0)
    const = lambda i: (0, 0)
    operands, in_specs = [x2], [pl.BlockSpec((tm, d), row)]
    if mix is not None:
        y2, w_out = mix
        kdim = y2.shape[1]
        operands += [y2, w_out]
        in_specs += [pl.BlockSpec((tm, kdim), row), _resident((kdim, d), const)]
    operands += [g, w_gu, w_gu, w_down]
    in_specs += [_resident((1, d), const),
                 _resident((d, f), const),
                 _resident((d, f), lambda i: (0, 1)),
                 _resident((f, d), const)]
    out_specs = [pl.BlockSpec((tm, d), row)]
    out_shape = [jax.ShapeDtypeStruct((n, d), F32)]
    casts = []
    if cast_next is not None:
        gu32, down32, li, si = cast_next
        casts += [(gu32, (li, si)), (down32, (li, si))]
    casts += [(w32, (lj,)) for w32, lj in cast_mixer]
    bf16_rows = 2 * SUBLANES
    for w32, lead in casts:
        nrows, cols = w32.shape[-2:]
        share = next(sh for sh in (1, 2, 4, 8)
                     if steps % sh == 0 and (nrows * sh) % (steps * bf16_rows) == 0)
        rows = nrows * share // steps
        operands.append(w32)
        in_specs.append(pl.BlockSpec((None,) * len(lead) + (rows, cols),
                                     lambda i, sh=share, lead=lead: lead + (i // sh, 0)))
        out_specs.append(pl.BlockSpec((rows, cols), lambda i, sh=share: (i // sh, 0)))
        out_shape.append(jax.ShapeDtypeStruct((nrows, cols), BF16))
    return pl.pallas_call(
        functools.partial(_ffn_kernel, mixed=mix is not None, casting=len(casts)),
        grid=(steps,),
        in_specs=in_specs,
        out_specs=out_specs,
        out_shape=out_shape,
        scratch_shapes=[pltpu.VMEM((tm, f), BF16)],
        compiler_params=_cparams("arbitrary"),
        name="ffn" if mix is None else "mix_ffn",
    )(*operands)


def _log_sigmoid(z):
    return jnp.minimum(z, 0.0) - jnp.log(1.0 + jnp.exp(-jnp.abs(z)))


def _gla_proj_kernel(x_ref, g_ref, wm_ref, w2_ref, bg_ref,
                     q_ref, k_ref, v_ref, go_ref, laf_ref, lab_ref, *, qk, vdim, rank, q_scale):
    dot = functools.partial(jnp.dot, preferred_element_type=F32)
    tm = x_ref.shape[0]
    sub = min(PROJ_SUB_ROWS, tm)
    for r0 in range(0, tm, sub):
        rows = slice(r0, r0 + sub)
        h = (_rms_scale(x_ref[rows, :]) * g_ref[...]).astype(BF16)
        q_ref[rows, :] = dot(h, wm_ref[:, 0:qk]) * q_scale
        k_ref[rows, :] = dot(h, wm_ref[:, qk:2 * qk])
        v_ref[rows, :] = dot(h, wm_ref[:, 2 * qk:2 * qk + vdim]).astype(BF16)
        go_ref[rows, :] = dot(h, wm_ref[:, 2 * qk + vdim:2 * qk + 2 * vdim])
        lr = dot(h, wm_ref[:, 2 * qk + 2 * vdim:2 * qk + 2 * vdim + 2 * rank])
        zf = dot(lr[:, 0:rank].astype(BF16), w2_ref[0]) + bg_ref[0:1, :]
        zb = dot(lr[:, rank:2 * rank].astype(BF16), w2_ref[1]) + bg_ref[1:2, :]
        laf_ref[rows, :] = _log_sigmoid(zf) * (1.0 / GLA_GATE_NORM)
        lab_ref[rows, :] = _log_sigmoid(zb) * (1.0 / GLA_GATE_NORM)


def _gla_proj(x2, g, w_in, w2, bg, *, tm, heads):
    n, d = x2.shape
    rank, qk = w2.shape[1], w2.shape[2]
    vdim = (w_in.shape[1] - 2 * qk - 2 * rank) // 2
    dk = qk // heads
    kern = functools.partial(_gla_proj_kernel, qk=qk, vdim=vdim, rank=rank, q_scale=dk ** -0.5)
    row = lambda i: (i, 0)
    const2 = lambda i: (0, 0)
    return pl.pallas_call(
        kern,
        grid=(n // tm,),
        in_specs=[
            pl.BlockSpec((tm, d), row),
            pl.BlockSpec((1, d), const2),
            _resident(w_in.shape, const2),
            pl.BlockSpec(w2.shape, lambda i: (0, 0, 0)),
            pl.BlockSpec(bg.shape, const2),
        ],
        out_specs=[
            pl.BlockSpec((tm, qk), row), pl.BlockSpec((tm, qk), row),
            pl.BlockSpec((tm, vdim), row), pl.BlockSpec((tm, vdim), row),
            pl.BlockSpec((tm, qk), row), pl.BlockSpec((tm, qk), row),
        ],
        out_shape=[
            jax.ShapeDtypeStruct((n, qk), F32), jax.ShapeDtypeStruct((n, qk), F32),
            jax.ShapeDtypeStruct((n, vdim), BF16), jax.ShapeDtypeStruct((n, vdim), F32),
            jax.ShapeDtypeStruct((n, qk), F32), jax.ShapeDtypeStruct((n, qk), F32),
        ],
        compiler_params=_cparams("parallel"),
        name="gla_proj",
    )(x2, g, w_in, w2, bg)


def _cumsum_rows(x, reverse):
    c, w = x.shape
    sub = SUBLANES
    t = c // sub
    x3 = x.reshape(t, sub, w)
    row = lax.broadcasted_iota(jnp.int32, (t, sub, w), 1)
    step = 1
    while step < sub:
        if reverse:
            x3 = x3 + jnp.where(row < sub - step, pltpu.roll(x3, sub - step, axis=1), 0.0)
        else:
            x3 = x3 + jnp.where(row >= step, pltpu.roll(x3, step, axis=1), 0.0)
        step *= 2
    order = range(t - 1, -1, -1) if reverse else range(t)
    edge = 0 if reverse else sub - 1
    tiles, carry = [None] * t, None
    for i in order:
        tile = x3[i] if carry is None else x3[i] + carry
        tiles[i] = tile
        carry = tile[edge:edge + 1, :]
    return jnp.concatenate(tiles, axis=0)


def _gla_chunk(q, k, v_bf, la, st, mask, reverse, scr, factorised):
    c = q.shape[0]
    contract_last = (((1,), (1,)), ((), ()))
    b = _cumsum_rows(la, reverse)
    end_row = 0 if reverse else c - 1
    b_end = b[end_row:end_row + 1, :]

    if factorised:
        b_first = b[c - 1 - end_row:c - end_row, :]
        mid = 0.5 * (b_first + b_end)
        qf = q * jnp.exp(b - mid)
        kf = k * jnp.exp(mid - b)
        q_dec = (qf * jnp.exp(mid)).astype(BF16)
        k_dec = (kf * jnp.exp(b_end - mid)).astype(BF16)
        scores = lax.dot_general(qf.astype(BF16), kf.astype(BF16), contract_last,
                                 preferred_element_type=F32)
    else:
        q_dec = (q * jnp.exp(b)).astype(BF16)
        k_dec = (k * jnp.exp(b_end - b)).astype(BF16)
        q_scr, k_scr, b_scr = scr
        q_scr[...] = q
        k_scr[...] = k
        b_scr[...] = b
        lane = lax.broadcasted_iota(jnp.int32, (c, c), 1)

        def col(j, a):
            bj = b_scr[pl.ds(j, 1), :]
            kj = k_scr[pl.ds(j, 1), :]
            t = q_scr[...] * jnp.exp(jnp.minimum(b_scr[...] - bj, 0.0)) * kj
            return jnp.where(lane == j, jnp.sum(t, axis=-1, keepdims=True), a)

        scores = lax.fori_loop(0, c, col, jnp.zeros((c, c), F32))

    scores = jnp.where(mask, scores, 0.0)
    intra = jnp.dot(scores.astype(BF16), v_bf, preferred_element_type=F32)
    inter = lax.dot_general(q_dec, st.astype(BF16), contract_last, preferred_element_type=F32)
    upd = lax.dot_general(v_bf, k_dec, (((0,), (0,)), ((), ())), preferred_element_type=F32)
    return inter + intra, st * jnp.exp(b_end) + upd


def _gla_core_kernel(q_ref, k_ref, v_ref, g_ref, laf_ref, lab_ref, ong_ref, o_ref,
                     of_scr, ob_scr, sf_scr, sb_scr, q_scr, k_scr, b_scr, *, chunk):
    s = q_ref.shape[1]
    n = s // chunk
    row = lax.broadcasted_iota(jnp.int32, (chunk, chunk), 0)
    colm = lax.broadcasted_iota(jnp.int32, (chunk, chunk), 1)
    mask_f = colm <= row
    mask_b = colm > row
    sf_scr[...] = jnp.zeros_like(sf_scr)
    sb_scr[...] = jnp.zeros_like(sb_scr)
    scr = (q_scr, k_scr, b_scr)

    def scan(factorised):
        def body(c, carry):
            rf = pl.ds(pl.multiple_of(c * chunk, chunk), chunk)
            rb = pl.ds(pl.multiple_of((n - 1 - c) * chunk, chunk), chunk)
            args_f = (q_ref[0, rf, :], k_ref[0, rf, :], v_ref[0, rf, :], laf_ref[0, rf, :], sf_scr[...])
            args_b = (q_ref[0, rb, :], k_ref[0, rb, :], v_ref[0, rb, :], lab_ref[0, rb, :], sb_scr[...])
            out_f, st_f = _gla_chunk(*args_f, mask_f, False, scr, factorised)
            out_b, st_b = _gla_chunk(*args_b, mask_b, True, scr, factorised)
            of_scr[rf, :] = out_f
            ob_scr[rb, :] = out_b
            sf_scr[...] = st_f
            sb_scr[...] = st_b
            return carry

        lax.fori_loop(0, n, body, 0, unroll=4 if factorised and n % 4 == 0 else 1)

    def chunk_decay_min(la_ref):
        return jnp.min(jnp.sum(la_ref[0].reshape(n, chunk, la_ref.shape[2]), axis=1))

    in_range = jnp.minimum(chunk_decay_min(laf_ref), chunk_decay_min(lab_ref)) >= -2.0 * GLA_HALF_RANGE

    @pl.when(in_range)
    def _():
        scan(True)

    @pl.when(jnp.logical_not(in_range))
    def _():
        scan(False)

    def fin(c, carry):
        r = pl.ds(pl.multiple_of(c * chunk, chunk), chunk)
        o = of_scr[r, :] + ob_scr[r, :]
        g = g_ref[0, r, :]
        y = _rms_scale(o) * ong_ref[...] * (g * jax.nn.sigmoid(g))
        o_ref[0, r, :] = y.astype(o_ref.dtype)
        return carry

    lax.fori_loop(0, n, fin, 0, unroll=2 if n % 2 == 0 else 1)


def _gla_core(q, k, v, go, laf, lab, ong, *, heads):
    bsz, s, qk = q.shape
    vdim = v.shape[2]
    dk, dv = qk // heads, vdim // heads
    chunk = min(GLA_CHUNK, s)
    blk = lambda w: pl.BlockSpec((1, s, w), lambda b, h: (b, 0, h))
    kern = functools.partial(_gla_core_kernel, chunk=chunk)
    return pl.pallas_call(
        kern,
        grid=(bsz, heads),
        in_specs=[blk(dk), blk(dk), blk(dv), blk(dv), blk(dk), blk(dk),
                  pl.BlockSpec((1, dv), lambda b, h: (0, 0))],
        out_specs=blk(dv),
        out_shape=jax.ShapeDtypeStruct((bsz, s, vdim), BF16),
        scratch_shapes=[
            pltpu.VMEM((s, dv), F32), pltpu.VMEM((s, dv), F32),
            pltpu.VMEM((dv, dk), F32), pltpu.VMEM((dv, dk), F32),
            pltpu.VMEM((chunk, dk), F32), pltpu.VMEM((chunk, dk), F32),
            pltpu.VMEM((chunk, dk), F32),
        ],
        compiler_params=_cparams("parallel", "parallel"),
        name="gla_core",
    )(q, k, v, go, laf, lab, ong)


def _group_mean_sq(t, bd_ref):
    cols = t.shape[1]
    parts = []
    for c0 in range(0, cols, LANES):
        sq = t[:, c0:c0 + LANES]
        parts.append(jnp.dot((sq * sq).astype(BF16), bd_ref[...], preferred_element_type=F32))
    return jnp.concatenate(parts, axis=1)


def _diff_proj_kernel(x_ref, g_ref, w_ref, gq_ref, gk_ref, bd_ref, q_ref, k_ref, vx_ref, *, qk, dv, q_scale):
    dot = functools.partial(jnp.dot, preferred_element_type=F32)
    tm = x_ref.shape[0]
    sub = min(PROJ_SUB_ROWS, tm)
    ones = jnp.ones((sub, dv), BF16)
    for r0 in range(0, tm, sub):
        rows = slice(r0, r0 + sub)
        h = (_rms_scale(x_ref[rows, :]) * g_ref[...]).astype(BF16)
        q = dot(h, w_ref[:, 0:qk])
        q_ref[rows, :] = (q * lax.rsqrt(_group_mean_sq(q, bd_ref) + EPS) * gq_ref[...] * q_scale).astype(BF16)
        k = dot(h, w_ref[:, qk:2 * qk])
        k_ref[rows, :] = (k * lax.rsqrt(_group_mean_sq(k, bd_ref) + EPS) * gk_ref[...]).astype(BF16)
        v = dot(h, w_ref[:, 2 * qk:]).astype(BF16)
        for hh in range(v.shape[1] // dv):
            vx_ref[rows, 2 * hh * dv:(2 * hh + 1) * dv] = v[:, hh * dv:(hh + 1) * dv]
            vx_ref[rows, (2 * hh + 1) * dv:(2 * hh + 2) * dv] = ones


def _diff_proj(x2, g, w, gq_row, gk_row, bd, *, tm, dh, dv):
    n, d = x2.shape
    qk = gq_row.shape[1]
    vdim = 2 * (w.shape[1] - 2 * qk)
    kern = functools.partial(_diff_proj_kernel, qk=qk, dv=dv, q_scale=dh ** -0.5)
    row = lambda i: (i, 0)
    const2 = lambda i: (0, 0)
    return pl.pallas_call(
        kern,
        grid=(n // tm,),
        in_specs=[
            pl.BlockSpec((tm, d), row),
            pl.BlockSpec((1, d), const2),
            _resident(w.shape, const2),
            pl.BlockSpec((1, qk), const2),
            pl.BlockSpec((1, qk), const2),
            pl.BlockSpec(bd.shape, const2),
        ],
        out_specs=[pl.BlockSpec((tm, qk), row), pl.BlockSpec((tm, qk), row), pl.BlockSpec((tm, vdim), row)],
        out_shape=[jax.ShapeDtypeStruct((n, qk), BF16), jax.ShapeDtypeStruct((n, qk), BF16),
                   jax.ShapeDtypeStruct((n, vdim), BF16)],
        compiler_params=_cparams("parallel"),
        name="diff_proj",
    )(x2, g, w, gq_row, gk_row, bd)


def _bucket_thresholds():
    nb = REL_BUCKETS // 2
    max_exact = nb // 2
    m = nb - max_exact
    return [int(math.ceil(max_exact * (REL_MAX_DIST / max_exact) ** (kk / m) - 1e-9)) for kk in range(1, m)]


def _rel_bias_kernel(table_ref, off_ref, o_ref, prof_ref, *, tr):
    heads, _, s = o_ref.shape
    nb = REL_BUCKETS // 2
    max_exact = nb // 2
    i0 = pl.program_id(0) * tr

    @pl.when(pl.program_id(0) == 0)
    def _():
        rel = lax.broadcasted_iota(jnp.int32, (1, 2 * s), 1) - s
        n = jnp.abs(rel)
        large = jnp.full((1, 2 * s), max_exact, jnp.int32)
        for thr in _bucket_thresholds():
            large = large + (n >= thr).astype(jnp.int32)
        bucket = jnp.where(rel > 0, nb, 0) + jnp.where(n < max_exact, n, large)
        for hh in range(heads):
            acc = jnp.zeros((1, 2 * s), F32)
            for u in range(REL_BUCKETS):
                acc = jnp.where(bucket == u, table_ref[u, hh], acc)
            prof_ref[hh:hh + 1, :] = acc - off_ref[hh]

    base = pl.multiple_of(s - tr - i0, tr)
    for hh in range(heads):
        w = prof_ref[hh:hh + 1, pl.ds(base, s + tr)]
        x = pltpu.roll(jnp.broadcast_to(w, (tr, s + tr)), 0, axis=1, stride=1, stride_axis=0)
        o_ref[hh] = x[:, tr:tr + s]


def _rel_bias(table, off, s, *, tr=LANES):
    heads = table.shape[1]
    return pl.pallas_call(
        functools.partial(_rel_bias_kernel, tr=tr),
        grid=(s // tr,),
        in_specs=[pl.BlockSpec(memory_space=pltpu.SMEM), pl.BlockSpec(memory_space=pltpu.SMEM)],
        out_specs=pl.BlockSpec((heads, tr, s), lambda i: (0, i, 0)),
        out_shape=jax.ShapeDtypeStruct((heads, s, s), F32),
        scratch_shapes=[pltpu.VMEM((heads, 2 * s), F32)],
        compiler_params=_cparams("arbitrary"),
        name="rel_bias",
    )(table, off)


def _diff_attn_kernel(flag_ref, q_ref, k_ref, vx_ref, bias_ref, lam_ref, sg_ref, o_ref, *, dh, lam_init):
    nb, tq = q_ref.shape[0], q_ref.shape[1]
    s = k_ref.shape[1]
    dv = vx_ref.shape[2] // 2
    lane = lax.broadcasted_iota(jnp.int32, (tq, 2 * dh), 1)
    zero = jnp.zeros((tq, 2 * dh), BF16)

    lv = lam_ref[...]
    lam = (jnp.exp(jnp.sum(lv[0:1] * lv[1:2], axis=-1, keepdims=True))
           - jnp.exp(jnp.sum(lv[2:3] * lv[3:4], axis=-1, keepdims=True)) + lam_init)

    def attend(exact_max):
        for bb in range(nb):
            q = q_ref[bb]
            qs = jnp.concatenate([jnp.where(lane < dh, q, zero), jnp.where(lane >= dh, q, zero)], axis=0)
            sc = lax.dot_general(qs, k_ref[bb], (((1,), (1,)), ((), ())), preferred_element_type=F32)
            x = sc.reshape(2, tq, s) + bias_ref[...]
            if exact_max:
                x = x - jnp.max(x, axis=-1, keepdims=True)
            e = jnp.exp(x).astype(BF16).reshape(2 * tq, s)
            pv = jnp.dot(e, vx_ref[bb], preferred_element_type=F32)
            r = pv[:, 0:dv] / pv[:, dv:2 * dv]
            o = r[0:tq] - lam * r[tq:2 * tq]
            o_ref[bb] = (_rms_scale(o) * sg_ref[...] * (1.0 - lam_init)).astype(o_ref.dtype)

    @pl.when(flag_ref[0] == 1)
    def _():
        attend(False)

    @pl.when(flag_ref[0] != 1)
    def _():
        attend(True)


def _diff_attn(flag, q, k, vx, bias, lam_vecs, subln_g, *, heads, tq, nb, lam_init):
    bsz, s, qkdim = q.shape
    dh = qkdim // heads // 2
    dv = vx.shape[2] // heads // 2
    kern = functools.partial(_diff_attn_kernel, dh=dh, lam_init=lam_init)
    return pl.pallas_call(
        kern,
        grid=(heads, s // tq, bsz // nb),
        in_specs=[
            pl.BlockSpec(memory_space=pltpu.SMEM),
            pl.BlockSpec((nb, tq, 2 * dh), lambda h, i, b: (b, i, h)),
            pl.BlockSpec((nb, s, 2 * dh), lambda h, i, b: (b, 0, h)),
            pl.BlockSpec((nb, s, 2 * dv), lambda h, i, b: (b, 0, h)),
            pl.BlockSpec((1, tq, s), lambda h, i, b: (h, i, 0)),
            pl.BlockSpec(lam_vecs.shape, lambda h, i, b: (0, 0)),
            pl.BlockSpec((1, dv), lambda h, i, b: (0, 0)),
        ],
        out_specs=pl.BlockSpec((nb, tq, dv), lambda h, i, b: (b, i, h)),
        out_shape=jax.ShapeDtypeStruct((bsz, s, heads * dv), BF16),
        compiler_params=_cparams("parallel", "parallel", "arbitrary"),
        name="diff_attn",
    )(flag, q, k, vx, bias, lam_vecs, subln_g)


def _pick_tile(n, pref):
    t = min(n, pref)
    while n % t:
        t //= 2
    return t


def kernel(x, norm_g, ffn_w_gu, ffn_w_down, gla_w_in, gla_w_gate2, gla_b_gate, gla_o_norm_g, gla_w_out,
           diff_w_in, diff_qk_norm_g, diff_lambda, diff_subln_g, diff_w_out, rel_bias_table):
    bsz, s, d = x.shape
    n = bsz * s
    depth = norm_g.shape[0]
    gla_dv = gla_o_norm_g.shape[1]
    gla_heads = gla_w_out.shape[1] // gla_dv
    dh = diff_qk_norm_g.shape[2]
    diff_heads = rel_bias_table.shape[1]
    diff_qk = diff_heads * 2 * dh
    diff_dv = diff_w_out.shape[1] // diff_heads

    tm = _pick_tile(n, PROJ_ROWS)
    tf = _pick_tile(n, FFN_ROWS)
    tq = _pick_tile(s, ATTN_Q_ROWS)
    nb = _pick_tile(bsz, ATTN_BATCH)

    w_gu = ffn_w_gu[0, 0].astype(BF16)
    w_down = ffn_w_down[0, 0].astype(BF16)
    gla_w2 = gla_w_gate2.astype(BF16)

    grp = jnp.arange(LANES) // dh
    bd = jnp.where(grp[:, None] == grp[None, :], 1.0 / dh, 0.0).astype(BF16)

    table = rel_bias_table.astype(F32)
    g_abs = jnp.max(jnp.abs(diff_qk_norm_g.astype(F32)), axis=-1)
    s_max = jnp.max(g_abs[:, 0] * g_abs[:, 1]) * (dh ** 0.5) * BF16_NORM_SLACK
    t_max, t_min = jnp.max(table, axis=0), jnp.min(table, axis=0)
    offset_ok = 2.0 * s_max + jnp.max(t_max - t_min) <= DIFF_MAX_GAP
    bias_off = jnp.where(offset_ok, s_max + t_max, 0.0)
    attn_flag = offset_ok.astype(jnp.int32).reshape(1)
    bias = _rel_bias(table, bias_off, s) if depth > 1 else None

    x2 = x.reshape(n, d)
    for i in range(depth):
        j = i // 2
        mixer_w = (gla_w_in, gla_w_out) if i % 2 == 0 else (diff_w_in, diff_w_out)
        x2, w_gu, w_down, w_in_b, w_out_b = _ffn(x2, norm_g[i, 0:1], w_gu, w_down, tm=tf,
                                                 cast_next=(ffn_w_gu, ffn_w_down, i, 1),
                                                 cast_mixer=[(w, j) for w in mixer_w])
        g_mix = norm_g[i, 1:2]
        r3 = lambda t: t.reshape(bsz, s, t.shape[1])
        if i % 2 == 0:
            q, k, v, go, laf, lab = _gla_proj(x2, g_mix, w_in_b, gla_w2[j], gla_b_gate[j],
                                              tm=tm, heads=gla_heads)
            y = _gla_core(r3(q), r3(k), r3(v), r3(go), r3(laf), r3(lab), gla_o_norm_g[j:j + 1],
                          heads=gla_heads)
            mix = (y.reshape(n, -1), w_out_b)
        else:
            lam_init = 0.8 - 0.6 * math.exp(-0.3 * i)
            gq = jnp.tile(diff_qk_norm_g[j, 0], diff_qk // dh)[None, :]
            gk = jnp.tile(diff_qk_norm_g[j, 1], diff_qk // dh)[None, :]
            q, k, vx = _diff_proj(x2, g_mix, w_in_b, gq, gk, bd, tm=tm, dh=dh, dv=diff_dv)
            y = _diff_attn(attn_flag, r3(q), r3(k), r3(vx), bias, diff_lambda[j], diff_subln_g[j:j + 1],
                           heads=diff_heads, tq=tq, nb=nb, lam_init=lam_init)
            mix = (y.reshape(n, -1), w_out_b)
        cast_next = (ffn_w_gu, ffn_w_down, i + 1, 0) if i + 1 < depth else None
        res = _ffn(x2, norm_g[i, 2:3], w_gu, w_down, tm=tf, mix=mix, cast_next=cast_next)
        if cast_next is not None:
            x2, w_gu, w_down = res
        else:
            x2, = res
    return x2.reshape(bsz, s, d)
```
